```python
import jax, jax.numpy as jnp
from jax import lax
import numpy as np

D_MODEL = 1024
BATCH = 4
SEQ = 8192
DEPTH = 4

N_MIXERS = 4
D_INNER = D_MODEL
RMS_EPS = 1e-6

RWKV_HEAD = 64
RWKV_HEADS = D_INNER // RWKV_HEAD
RWKV_DECAY_LORA = 64
RWKV_AAA_LORA = 64
RWKV_GN_EPS = 64e-5
RWKV_COLS = 4 * D_INNER + RWKV_DECAY_LORA + RWKV_AAA_LORA
RWKV_SPLITS = (D_INNER, 2 * D_INNER, 3 * D_INNER, 3 * D_INNER + RWKV_DECAY_LORA,
               3 * D_INNER + RWKV_DECAY_LORA + RWKV_AAA_LORA)

HGRN_EXPAND = 128
HGRN_HEADS = D_INNER // HGRN_EXPAND
HGRN_HEAD_V = D_INNER // HGRN_HEADS
HGRN_CHUNK = 32
HGRN_COLS = 4 * D_INNER

CONV_WIDTH = 3
CONV_COLS = 4 * D_INNER

GMLP_CHUNK = 128
GMLP_GROUPS = 8
GMLP_GROUP_CH = D_INNER // GMLP_GROUPS
GMLP_COLS = 3 * D_INNER

N_RWKV = (DEPTH + 3) // 4
N_HGRN = (DEPTH + 2) // 4
N_CONV = (DEPTH + 1) // 4
N_GMLP = DEPTH // 4

kernel_name = "hybrid_rwkv7_hgrn2_shortconv_gmlp_interleaved"


def rmsnorm(x, g):
    xf = x.astype(jnp.float32)
    y = xf * lax.rsqrt(jnp.mean(xf * xf, axis=-1, keepdims=True) + RMS_EPS)
    return (y * g.astype(jnp.float32)).astype(x.dtype)


def shift1(z):
    return jnp.pad(z, ((0, 0), (1, 0), (0, 0)))[:, :-1]


def rwkv7_time_mix(h, w_in, mu, w0, w_w2, a0, w_a2, k_k, k_a, r_k, gn_g, gn_b, w_out):
    B, T, _ = h.shape
    H, N = RWKV_HEADS, RWKV_HEAD
    f32 = jnp.float32
    p = h @ w_in
    p = p + (shift1(p) - p) * mu
    r, k, v, wd, ad, gate = jnp.split(p, RWKV_SPLITS, axis=-1)
    w_log = -jax.nn.softplus(-(w0 + jnp.tanh(wd) @ w_w2).astype(f32)) - 0.5
    decay = jnp.exp(-jnp.exp(w_log))
    a = jax.nn.sigmoid((a0 + ad @ w_a2).astype(f32))
    r = r.astype(f32)
    k = k.astype(f32)
    v = v.astype(f32)
    kk = (k * k_k).reshape(B, T, H, N)
    kk = kk / jnp.maximum(jnp.linalg.norm(kk, axis=-1, keepdims=True), 1e-12)
    k = k * (1.0 + (a - 1.0) * k_a)
    r_h, k_h, v_h, w_h, a_h = (z.reshape(B, T, H, N) for z in (r, k, v, decay, a))
    aa = -kk
    bb = kk * a_h

    def step(S, inp):
        r_t, w_t, k_t, v_t, aa_t, bb_t = inp
        sa = jnp.einsum('bhvk,bhk->bhv', S, aa_t)
        S = S * w_t[:, :, None, :] + sa[..., None] * bb_t[:, :, None, :] + v_t[..., None] * k_t[:, :, None, :]
        return S, jnp.einsum('bhvk,bhk->bhv', S, r_t)

    xs = tuple(jnp.moveaxis(z, 1, 0) for z in (r_h, w_h, k_h, v_h, aa, bb))
    _, y = lax.scan(step, jnp.zeros((B, H, N, N), f32), xs)
    y = jnp.moveaxis(y, 0, 1)
    mean = jnp.mean(y, axis=-1, keepdims=True)
    var = jnp.mean(jnp.square(y - mean), axis=-1, keepdims=True)
    y = ((y - mean) * lax.rsqrt(var + RWKV_GN_EPS)).reshape(B, T, D_INNER) * gn_g + gn_b
    bonus = jnp.sum(r_h * k_h * r_k, axis=-1, keepdims=True) * v_h
    y = (y + bonus.reshape(B, T, D_INNER)) * jax.nn.silu(gate.astype(f32))
    return y.astype(h.dtype) @ w_out


def chunk_gated_linear(q, k, v, g, chunk):
    B, T, H, DK = q.shape
    DV = v.shape[-1]
    NC = T // chunk

    def blk(z):
        return z.reshape(B, NC, chunk, H, z.shape[-1]).transpose(0, 3, 1, 2, 4)

    q, k, v, g = (blk(z.astype(jnp.float32)) for z in (q, k, v, g))
    b = jnp.cumsum(g, axis=3)
    m = b[:, :, :, chunk // 2:chunk // 2 + 1]
    qm = q * jnp.exp(b - m)
    km = k * jnp.exp(m - b)
    att = jnp.einsum('bhnck,bhnsk->bhncs', qm, km)
    causal = jnp.tril(jnp.ones((chunk, chunk), dtype=bool))
    att = jnp.where(causal, att, 0.0)
    o_intra = jnp.einsum('bhncs,bhnsv->bhncv', att, v)
    b_last = b[:, :, :, -1]
    q_in = q * jnp.exp(b)
    k_out = k * jnp.exp(b_last[:, :, :, None, :] - b)

    def step(S, inp):
        q_c, k_c, v_c, d_c = inp
        o = jnp.einsum('bhck,bhkv->bhcv', q_c, S)
        S = S * jnp.exp(d_c)[..., None] + jnp.einsum('bhck,bhcv->bhkv', k_c, v_c)
        return S, o

    xs = tuple(jnp.moveaxis(z, 2, 0) for z in (q_in, k_out, v, b_last))
    _, o_inter = lax.scan(step, jnp.zeros((B, H, DK, DV), jnp.float32), xs)
    o = o_intra + jnp.moveaxis(o_inter, 0, 2)
    return o.transpose(0, 2, 3, 1, 4).reshape(B, T, H, DV)


def hgrn2_mix(h, lb, w_in, gn_g, w_out):
    B, T, _ = h.shape
    f32 = jnp.float32
    p = h @ w_in
    q, f_pre, i_in, gate = jnp.split(p, 4, axis=-1)
    f_pre = f_pre.astype(f32)
    lb = lb.astype(f32)
    log_f = jnp.logaddexp(jnp.log(lb), jnp.log1p(-lb) + jax.nn.log_sigmoid(f_pre))
    k = (1.0 - lb) * jax.nn.sigmoid(-f_pre)
    heads = lambda z: z.reshape(B, T, HGRN_HEADS, -1)
    o = chunk_gated_linear(heads(q), heads(k), heads(i_in), heads(log_f), HGRN_CHUNK)
    o = o * lax.rsqrt(jnp.mean(o * o, axis=-1, keepdims=True) + RMS_EPS)
    o = o.reshape(B, T, D_INNER) * gn_g * jax.nn.silu(gate.astype(f32))
    return o.astype(h.dtype) @ w_out


def hgrn_lower_bound(lb_logits, layer):
    cb = jnp.cumsum(jax.nn.softmax(lb_logits.astype(jnp.float32), axis=0), axis=0)
    return cb[layer] - cb[0]


def short_conv_mix(h, w_in, conv_w, w_out):
    T = h.shape[1]
    p = h @ w_in
    b_gate, c_gate, z, gate = jnp.split(p, 4, axis=-1)
    y = c_gate * z
    yp = jnp.pad(y, ((0, 0), (CONV_WIDTH - 1, 0), (0, 0)))
    yc = sum(conv_w[j] * yp[:, j:j + T] for j in range(CONV_WIDTH))
    return (b_gate * yc * jax.nn.silu(gate)) @ w_out


def gmlp_chunk_mix(h, w_in, v_g, w_s, b_s, w_out):
    B, T, _ = h.shape
    NC = T // GMLP_CHUNK
    p = h @ w_in
    u, v, gate = jnp.split(p, 3, axis=-1)
    v = rmsnorm(v, v_g)
    vb = v.reshape(B, NC, GMLP_CHUNK, GMLP_GROUPS, GMLP_GROUP_CH)
    ws = w_s * jnp.tril(jnp.ones((GMLP_CHUNK, GMLP_CHUNK), w_s.dtype))
    s = jnp.einsum('gts,bnsgc->bntgc', ws, vb) + b_s.T[None, None, :, :, None]
    s = s.reshape(B, T, D_INNER)
    return (u * s * jax.nn.silu(gate)) @ w_out


def setup_inputs(seed: int = 0) -> dict:
    key = jax.random.key(seed)
    ks = iter(jax.random.split(key, 40))
    f32 = jnp.float32

    def nrm(shape, scale):
        return jax.random.normal(next(ks), shape, f32) * scale

    D, DI = D_MODEL, D_INNER
    return {
        "x": nrm((BATCH, SEQ, D), 1.0),
        "norm_g": 1.0 + nrm((DEPTH, D), 0.02),
        "final_g": 1.0 + nrm((D,), 0.02),
        "rwkv_w_in": nrm((N_RWKV, D, RWKV_COLS), D ** -0.5),
        "rwkv_mu": jax.random.uniform(next(ks), (N_RWKV, RWKV_COLS), f32),
        "rwkv_w0": nrm((N_RWKV, DI), 0.5),
        "rwkv_w_w2": nrm((N_RWKV, RWKV_DECAY_LORA, DI), 0.5 * RWKV_DECAY_LORA ** -0.5),
        "rwkv_a0": nrm((N_RWKV, DI), 0.5),
        "rwkv_w_a2": nrm((N_RWKV, RWKV_AAA_LORA, DI), RWKV_AAA_LORA ** -0.5),
        "rwkv_k_k": 0.85 + nrm((N_RWKV, DI), 0.05),
        "rwkv_k_a": 1.0 + nrm((N_RWKV, DI), 0.05),
        "rwkv_r_k": nrm((N_RWKV, RWKV_HEADS, RWKV_HEAD), 0.1),
        "rwkv_gn_g": 1.0 + nrm((N_RWKV, DI), 0.02),
        "rwkv_gn_b": nrm((N_RWKV, DI), 0.02),
        "rwkv_w_out": nrm((N_RWKV, DI, D), DI ** -0.5),
        "hgrn_lb_logits": nrm((DEPTH, DI), 0.1),
        "hgrn_w_in": nrm((N_HGRN, D, HGRN_COLS), D ** -0.5),
        "hgrn_gn_g": 1.0 + nrm((N_HGRN, DI), 0.02),
        "hgrn_w_out": nrm((N_HGRN, DI, D), DI ** -0.5),
        "conv_w_in": nrm((N_CONV, D, CONV_COLS), D ** -0.5),
        "conv_w": nrm((N_CONV, CONV_WIDTH, DI), CONV_WIDTH ** -0.5),
        "conv_w_out": nrm((N_CONV, DI, D), DI ** -0.5),
        "gmlp_w_in": nrm((N_GMLP, D, GMLP_COLS), D ** -0.5),
        "gmlp_v_g": 1.0 + nrm((N_GMLP, DI), 0.02),
        "gmlp_w_s": nrm((N_GMLP, GMLP_GROUPS, GMLP_CHUNK, GMLP_CHUNK), GMLP_CHUNK ** -0.5),
        "gmlp_b_s": 1.0 + nrm((N_GMLP, GMLP_GROUPS, GMLP_CHUNK), 0.1),
        "gmlp_w_out": nrm((N_GMLP, DI, D), DI ** -0.5),
    }


def reference(x, norm_g, final_g,
              rwkv_w_in, rwkv_mu, rwkv_w0, rwkv_w_w2, rwkv_a0, rwkv_w_a2, rwkv_k_k, rwkv_k_a,
              rwkv_r_k, rwkv_gn_g, rwkv_gn_b, rwkv_w_out,
              hgrn_lb_logits, hgrn_w_in, hgrn_gn_g, hgrn_w_out,
              conv_w_in, conv_w, conv_w_out,
              gmlp_w_in, gmlp_v_g, gmlp_w_s, gmlp_b_s, gmlp_w_out):
    h = x
    for i in range(DEPTH):
        hn = rmsnorm(h, norm_g[i])
        m, j = i % N_MIXERS, i // N_MIXERS
        if m == 0:
            y = rwkv7_time_mix(hn, rwkv_w_in[j], rwkv_mu[j], rwkv_w0[j], rwkv_w_w2[j], rwkv_a0[j],
                               rwkv_w_a2[j], rwkv_k_k[j], rwkv_k_a[j], rwkv_r_k[j],
                               rwkv_gn_g[j], rwkv_gn_b[j], rwkv_w_out[j])
        elif m == 1:
            y = hgrn2_mix(hn, hgrn_lower_bound(hgrn_lb_logits, i), hgrn_w_in[j], hgrn_gn_g[j], hgrn_w_out[j])
        elif m == 2:
            y = short_conv_mix(hn, conv_w_in[j], conv_w[j], conv_w_out[j])
        else:
            y = gmlp_chunk_mix(hn, gmlp_w_in[j], gmlp_v_g[j], gmlp_w_s[j], gmlp_b_s[j], gmlp_w_out[j])
        h = h + y.astype(h.dtype)
    return rmsnorm(h, final_g)
```

```python
import functools

import jax
import jax.numpy as jnp
from jax import lax
from jax.experimental import pallas as pl
from jax.experimental.pallas import tpu as pltpu

F32 = jnp.float32
BF16 = jnp.bfloat16

RMS_EPS = 1e-6
RWKV_HEAD = 64
RWKV_GN_EPS = 64e-5
RWKV_LORA = 64
HGRN_HEAD = 128
GMLP_CHUNK = 128
GMLP_GROUP = 128
CONV_WIDTH = 3

LANES = 128
SUBLANES = 8
CHUNK = 64
HALF = CHUNK // 2
VMEM_LIMIT = 56 * 1024 * 1024


def _mm(a, b):
    return jnp.dot(a.astype(BF16), b.astype(BF16), preferred_element_type=F32)


def _mm_nt(a, b):
    return lax.dot_general(a.astype(BF16), b.astype(BF16), (((1,), (1,)), ((), ())),
                           preferred_element_type=F32)


def _mm_tn(a, b):
    return lax.dot_general(a.astype(BF16), b.astype(BF16), (((0,), (0,)), ((), ())),
                           preferred_element_type=F32)


def _split2(x):
    hi = x.astype(BF16)
    lo = (x - hi.astype(F32)).astype(BF16)
    return hi, lo


def _split3(x):
    hi = x.astype(BF16)
    r1 = x - hi.astype(F32)
    mid = r1.astype(BF16)
    lo = (r1 - mid.astype(F32)).astype(BF16)
    return hi, mid, lo


def _dot_exact_rhs(x, e):
    hi, lo = _split2(x)
    return (jnp.dot(hi, e, preferred_element_type=F32) + jnp.dot(lo, e, preferred_element_type=F32))


def _cumsum_rows(tri, g):
    hi, mid, lo = _split3(g)
    return (jnp.dot(tri, hi, preferred_element_type=F32) + jnp.dot(tri, mid, preferred_element_type=F32)
            + jnp.dot(tri, lo, preferred_element_type=F32))


def _sigmoid(x):
    return 1.0 / (1.0 + jnp.exp(-x))


def _silu(x):
    return x * _sigmoid(x)


def _softplus(x):
    return jnp.maximum(x, 0.0) + jnp.log1p(jnp.exp(-jnp.abs(x)))


def _rmsnorm(x, g):
    ms = jnp.mean(x * x, axis=-1, keepdims=True)
    return x * lax.rsqrt(ms + RMS_EPS) * g


def _iota2(shape, axis):
    return lax.broadcasted_iota(jnp.int32, shape, axis)


def _tri_ones(n):
    return (_iota2((n, n), 0) >= _iota2((n, n), 1)).astype(BF16)


def _row_spec(tm, d):
    return pl.BlockSpec((None, tm, d), lambda b, t: (b, t, 0))


def _full_spec(shape):
    n = len(shape)
    return pl.BlockSpec(shape, lambda *_: (0,) * n)


def _params(n_axes):
    return pltpu.CompilerParams(dimension_semantics=("arbitrary",) * n_axes, vmem_limit_bytes=VMEM_LIMIT)


def _rwkv_prep_body(h_ref, ng_ref, win_ref, mu_ref, w0_ref, ww_ref, a0_ref, wa_ref, kk_ref, ka_ref,
                    e_ref, et_ref,
                    r_out, k_out, v_out, g_out, kk_out, bv_out, gate_out, carry_ref, *, d):
    tm = h_ref.shape[0]

    @pl.when(pl.program_id(1) == 0)
    def _():
        carry_ref[...] = jnp.zeros_like(carry_ref)

    hn = _rmsnorm(h_ref[...], ng_ref[...])
    p = _mm(hn, win_ref[...])
    row = _iota2(p.shape, 0)
    prev = jnp.where(row == 0, carry_ref[SUBLANES - 1:SUBLANES, :], pltpu.roll(p, 1, 0))
    carry_ref[...] = p[tm - SUBLANES:tm, :]
    p = p + (prev - p) * mu_ref[...]

    r = p[:, 0:d]
    k = p[:, d:2 * d]
    v = p[:, 2 * d:3 * d]
    lw = p[:, 3 * d:3 * d + 2 * RWKV_LORA]
    gate = p[:, 3 * d + 2 * RWKV_LORA:]
    lw = jnp.where(_iota2(lw.shape, 1) < RWKV_LORA, jnp.tanh(lw), lw)
    w_log = -_softplus(-(w0_ref[...] + _mm(lw, ww_ref[...]))) - 0.5
    a = _sigmoid(a0_ref[...] + _mm(lw, wa_ref[...]))
    kk = k * kk_ref[...]
    n2 = _dot_exact_rhs(_dot_exact_rhs(kk * kk, e_ref[...]), et_ref[...])
    kk = kk / jnp.maximum(jnp.sqrt(n2), 1e-12)
    r_out[...] = r
    k_out[...] = k * (1.0 + (a - 1.0) * ka_ref[...])
    v_out[...] = v
    g_out[...] = -jnp.exp(w_log)
    kk_out[...] = kk
    bv_out[...] = kk * a
    gate_out[...] = gate


def _rwkv_prep(h, ng, w_in, mu, w0, ww_pad, a0, wa_pad, k_k, k_a, e, et, *, tm):
    bsz, t, d = h.shape
    cols = w_in.shape[1]
    row = _row_spec(tm, d)
    vec = _full_spec((1, d))
    out = jax.ShapeDtypeStruct((bsz, t, d), F32)
    return pl.pallas_call(
        functools.partial(_rwkv_prep_body, d=d),
        grid=(bsz, t // tm),
        in_specs=[row, vec, _full_spec(w_in.shape), _full_spec((1, cols)), vec, _full_spec(ww_pad.shape), vec,
                  _full_spec(wa_pad.shape), vec, vec, _full_spec(e.shape), _full_spec(et.shape)],
        out_specs=[row] * 7,
        out_shape=[out] * 7,
        scratch_shapes=[pltpu.VMEM((SUBLANES, cols), F32)],
        compiler_params=_params(2),
        name="rwkv_prep",
    )(h, ng, w_in, mu, w0, ww_pad, a0, wa_pad, k_k, k_a, e, et)


def _stack_heads(x, m0):
    return jnp.concatenate([jnp.where(m0, x, 0.0), jnp.where(m0, 0.0, x)], axis=0)


def _unit_lower_inverse(low, i2, j2):
    s = 1
    t_inv = jnp.where(i2 == j2, 1.0, 0.0)
    while s < CHUNK:
        sub = jnp.where(((i2 // (2 * s)) == (j2 // (2 * s))) & ((i2 // s) != (j2 // s)), low, 0.0)
        t_inv = t_inv + (sub if s == 1 else _mm(_mm(t_inv, sub), t_inv))
        s *= 2
    return t_inv


def _rwkv_scan_body(r_ref, k_ref, v_ref, g_ref, kk_ref, bv_ref, y_ref, state_ref):
    tm = r_ref.shape[0]
    c = CHUNK
    w = 2 * RWKV_HEAD

    @pl.when(pl.program_id(2) == 0)
    def _():
        state_ref[...] = jnp.zeros_like(state_ref)

    m0 = _iota2((c, w), 1) < RWKV_HEAD
    tri = _tri_ones(c)
    i2 = _iota2((2 * c, w), 0)
    j2 = _iota2((2 * c, w), 1)
    same_head = (i2 // RWKV_HEAD) == (j2 // RWKV_HEAD)
    ir = _iota2((c, w), 0)
    jr = _iota2((c, w), 1) % c
    stack = functools.partial(_stack_heads, m0=m0)

    h_state = state_ref[...]
    for ci in range(tm // c):
        rows = pl.ds(ci * c, c)
        r, k, v, g = r_ref[rows, :], k_ref[rows, :], v_ref[rows, :], g_ref[rows, :]
        kk, bv = kk_ref[rows, :], bv_ref[rows, :]
        bc = _cumsum_rows(tri, g)
        bl = bc[c - 1:c, :]
        p_inv = jnp.exp(-bc)
        p_end = jnp.exp(bl - bc)
        at = -kk * jnp.exp(bc - g)
        rt = r * jnp.exp(bc)
        bt2 = stack(bv * p_inv)
        kt2 = stack(k * p_inv)
        bh = bv * p_end
        kh = k * p_end
        at2 = stack(at)
        v2 = stack(v)

        low = jnp.where(j2 < i2, _mm_nt(at2, bt2), 0.0)
        t_inv = _unit_lower_inverse(low, i2, j2)
        t_row = t_inv[0:c, :] + t_inv[c:2 * c, :]
        w1 = _mm(t_row, at2)
        a_ak = jnp.where(jr < ir, _mm_nt(at, kt2), 0.0)
        w2 = _mm(t_row, stack(_mm(a_ak, v2)))
        a_rb = jnp.where(jr <= ir, _mm_nt(rt, bt2), 0.0)
        a_rk = jnp.where(jr <= ir, _mm_nt(rt, kt2), 0.0)
        q1 = rt + _mm(a_rb, stack(w1))
        q2 = _mm(a_rb, stack(w2)) + _mm(a_rk, v2)
        trans = (jnp.where(same_head, _mm_tn(bh, w1), 0.0)
                 + jnp.where(i2 == j2, jnp.broadcast_to(jnp.exp(bl), (2 * c, w)), 0.0))
        inject = jnp.where(same_head, _mm_tn(bh, w2) + _mm_tn(kh, v), 0.0)

        y_ref[rows, :] = _mm(q1, h_state) + q2
        h_state = _mm(trans, h_state) + inject
    state_ref[...] = h_state


def _rwkv_scan(r, k, v, g, kk, bv, *, tm):
    bsz, t, d = r.shape
    w = 2 * RWKV_HEAD
    spec = pl.BlockSpec((None, tm, w), lambda b, p, i: (b, i, p))
    return pl.pallas_call(
        _rwkv_scan_body,
        grid=(bsz, d // w, t // tm),
        in_specs=[spec] * 6,
        out_specs=spec,
        out_shape=jax.ShapeDtypeStruct((bsz, t, d), F32),
        scratch_shapes=[pltpu.VMEM((w, w), F32)],
        compiler_params=_params(3),
        name="rwkv_scan",
    )(r, k, v, g, kk, bv)


def _rwkv_out_body(h_ref, y_ref, r_ref, k_ref, v_ref, gate_ref, rk_ref, gg_ref, gb_ref, wout_ref,
                   e_ref, et_ref, o_ref):
    seg = lambda z: _dot_exact_rhs(_dot_exact_rhs(z, e_ref[...]), et_ref[...])
    y = y_ref[...]
    mean = seg(y) * (1.0 / RWKV_HEAD)
    dy = y - mean
    var = seg(dy * dy) * (1.0 / RWKV_HEAD)
    yn = dy * lax.rsqrt(var + RWKV_GN_EPS) * gg_ref[...] + gb_ref[...]
    bonus = seg(r_ref[...] * k_ref[...] * rk_ref[...]) * v_ref[...]
    out = (yn + bonus) * _silu(gate_ref[...])
    o_ref[...] = h_ref[...] + _mm(out, wout_ref[...])


def _rwkv_out(h, y, r, k, v, gate, r_k, gn_g, gn_b, w_out, e, et, *, tm):
    bsz, t, d = h.shape
    row = _row_spec(tm, d)
    vec = _full_spec((1, d))
    return pl.pallas_call(
        _rwkv_out_body,
        grid=(bsz, t // tm),
        in_specs=[row] * 6 + [vec, vec, vec, _full_spec(w_out.shape), _full_spec(e.shape), _full_spec(et.shape)],
        out_specs=row,
        out_shape=jax.ShapeDtypeStruct((bsz, t, d), F32),
        compiler_params=_params(2),
        name="rwkv_out",
    )(h, y, r, k, v, gate, r_k, gn_g, gn_b, w_out, e, et)


def _hgrn_body(h_ref, ng_ref, win_ref, lbl_ref, gg_ref, wout_ref, o_ref, state_ref, *, layer):
    tm, d = h_ref.shape
    c = CHUNK
    nh = d // HGRN_HEAD

    @pl.when(pl.program_id(1) == 0)
    def _():
        state_ref[...] = jnp.zeros_like(state_ref)

    x = h_ref[...]
    p = _mm(_rmsnorm(x, ng_ref[...]), win_ref[...])
    q = p[:, 0:d]
    f_pre = p[:, d:2 * d]
    v = p[:, 2 * d:3 * d]
    gate = p[:, 3 * d:4 * d]

    logits = lbl_ref[...]
    ex = jnp.exp(logits - jnp.max(logits, axis=0, keepdims=True))
    lb = jnp.sum(ex[1:layer + 1, :], axis=0, keepdims=True) / jnp.sum(ex, axis=0, keepdims=True)
    log_sig = jnp.minimum(f_pre, 0.0) - jnp.log1p(jnp.exp(-jnp.abs(f_pre)))
    t1 = jnp.log(lb)
    t2 = jnp.log1p(-lb) + log_sig
    log_f = jnp.maximum(t1, t2) + jnp.log1p(jnp.exp(-jnp.abs(t1 - t2)))
    k = (1.0 - lb) * _sigmoid(-f_pre)

    tri = _tri_ones(c)
    causal = _iota2((HALF, HALF), 0) >= _iota2((HALF, HALF), 1)
    head_out = []
    for hi in range(nh):
        cols = slice(hi * HGRN_HEAD, (hi + 1) * HGRN_HEAD)
        s_t = state_ref[hi]
        chunk_out = []
        for ci in range(tm // c):
            rows = slice(ci * c, (ci + 1) * c)
            qc, kc, vc, gc = q[rows, cols], k[rows, cols], v[rows, cols], log_f[rows, cols]
            b = _cumsum_rows(tri, gc)
            b_a, b_b = b[0:HALF], b[HALF:c]
            m_a = b[HALF // 2:HALF // 2 + 1]
            m_b = b[HALF + HALF // 2:HALF + HALF // 2 + 1]
            beta = b[HALF - 1:HALF]
            bl = b[c - 1:c]
            q_a, q_b, k_a, k_b, v_a, v_b = qc[0:HALF], qc[HALF:c], kc[0:HALF], kc[HALF:c], vc[0:HALF], vc[HALF:c]
            att_aa = jnp.where(causal, _mm_nt(q_a * jnp.exp(b_a - m_a), k_a * jnp.exp(m_a - b_a)), 0.0)
            att_bb = jnp.where(causal, _mm_nt(q_b * jnp.exp(b_b - m_b), k_b * jnp.exp(m_b - b_b)), 0.0)
            att_ba = _mm_nt(q_b * jnp.exp(b_b - beta), k_a * jnp.exp(beta - b_a))
            o = jnp.concatenate([_mm(att_aa, v_a), _mm(att_bb, v_b) + _mm(att_ba, v_a)], axis=0)
            o = o + _mm_nt(qc * jnp.exp(b), s_t)
            s_t = s_t * jnp.exp(bl) + _mm_tn(vc, kc * jnp.exp(bl - b))
            chunk_out.append(o * lax.rsqrt(jnp.mean(o * o, axis=-1, keepdims=True) + RMS_EPS))
        state_ref[hi] = s_t
        head_out.append(jnp.concatenate(chunk_out, axis=0))
    o_all = jnp.concatenate(head_out, axis=1) * gg_ref[...] * _silu(gate)
    o_ref[...] = x + _mm(o_all, wout_ref[...])


def _hgrn_layer(h, ng, w_in, lb_logits, gn_g, w_out, *, layer, tm):
    bsz, t, d = h.shape
    row = _row_spec(tm, d)
    vec = _full_spec((1, d))
    return pl.pallas_call(
        functools.partial(_hgrn_body, layer=layer),
        grid=(bsz, t // tm),
        in_specs=[row, vec, _full_spec(w_in.shape), _full_spec(lb_logits.shape), vec, _full_spec(w_out.shape)],
        out_specs=row,
        out_shape=jax.ShapeDtypeStruct((bsz, t, d), F32),
        scratch_shapes=[pltpu.VMEM((d // HGRN_HEAD, HGRN_HEAD, HGRN_HEAD), F32)],
        compiler_params=_params(2),
        name="hgrn_layer",
    )(h, ng, w_in, lb_logits, gn_g, w_out)


def _conv_body(h_ref, ng_ref, win_ref, cw_ref, wout_ref, o_ref, carry_ref):
    tm, d = h_ref.shape

    @pl.when(pl.program_id(1) == 0)
    def _():
        carry_ref[...] = jnp.zeros_like(carry_ref)

    x = h_ref[...]
    p = _mm(_rmsnorm(x, ng_ref[...]), win_ref[...])
    b_gate = p[:, 0:d]
    y = p[:, d:2 * d] * p[:, 2 * d:3 * d]
    gate = p[:, 3 * d:4 * d]
    row = _iota2((tm, d), 0)
    last = carry_ref[SUBLANES - 1:SUBLANES, :]
    last2 = carry_ref[SUBLANES - 2:SUBLANES - 1, :]
    y1 = jnp.where(row == 0, last, pltpu.roll(y, 1, 0))
    y2 = jnp.where(row == 0, last2, jnp.where(row == 1, last, pltpu.roll(y, 2, 0)))
    carry_ref[...] = y[tm - SUBLANES:tm, :]
    yc = cw_ref[0:1, :] * y2 + cw_ref[1:2, :] * y1 + cw_ref[2:3, :] * y
    o_ref[...] = x + _mm(b_gate * yc * _silu(gate), wout_ref[...])


def _conv_layer(h, ng, w_in, conv_w, w_out, *, tm):
    bsz, t, d = h.shape
    row = _row_spec(tm, d)
    return pl.pallas_call(
        _conv_body,
        grid=(bsz, t // tm),
        in_specs=[row, _full_spec((1, d)), _full_spec(w_in.shape), _full_spec(conv_w.shape), _full_spec(w_out.shape)],
        out_specs=row,
        out_shape=jax.ShapeDtypeStruct((bsz, t, d), F32),
        scratch_shapes=[pltpu.VMEM((SUBLANES, d), F32)],
        compiler_params=_params(2),
        name="conv_layer",
    )(h, ng, w_in, conv_w, w_out)


def _gmlp_body(h_ref, ng_ref, win_ref, vg_ref, ws_ref, bs_ref, wout_ref, fg_ref, o_ref):
    tm, d = h_ref.shape
    c = GMLP_CHUNK
    x = h_ref[...]
    p = _mm(_rmsnorm(x, ng_ref[...]), win_ref[...])
    u = p[:, 0:d]
    v = _rmsnorm(p[:, d:2 * d], vg_ref[...])
    gate = p[:, 2 * d:3 * d]
    causal = _iota2((c, c), 0) >= _iota2((c, c), 1)
    bias = bs_ref[...]
    chunk_out = []
    for ci in range(tm // c):
        rows = slice(ci * c, (ci + 1) * c)
        parts = []
        for gi in range(d // GMLP_GROUP):
            cols = slice(gi * GMLP_GROUP, (gi + 1) * GMLP_GROUP)
            parts.append(_mm(jnp.where(causal, ws_ref[gi], 0.0), v[rows, cols]))
        chunk_out.append(jnp.concatenate(parts, axis=1) + bias)
    s = jnp.concatenate(chunk_out, axis=0)
    out = x + _mm(u * s * _silu(gate), wout_ref[...])
    o_ref[...] = _rmsnorm(out, fg_ref[...])


def _gmlp_layer(h, ng, w_in, v_g, w_s, bs_full, w_out, final_g, *, tm):
    bsz, t, d = h.shape
    row = _row_spec(tm, d)
    vec = _full_spec((1, d))
    return pl.pallas_call(
        _gmlp_body,
        grid=(bsz, t // tm),
        in_specs=[row, vec, _full_spec(w_in.shape), vec, _full_spec(w_s.shape), _full_spec(bs_full.shape),
                  _full_spec(w_out.shape), vec],
        out_specs=row,
        out_shape=jax.ShapeDtypeStruct((bsz, t, d), F32),
        compiler_params=_params(2),
        name="gmlp_layer",
    )(h, ng, w_in, v_g, w_s, bs_full, w_out, final_g)


def _rwkv_layer(h, ng, w_in, mu, w0, w_w2, a0, w_a2, k_k, k_a, r_k, gn_g, gn_b, w_out):
    d = h.shape[-1]
    nh = d // RWKV_HEAD
    zeros = jnp.zeros((RWKV_LORA, d), F32)
    ww_pad = jnp.concatenate([w_w2, zeros], axis=0).astype(BF16)
    wa_pad = jnp.concatenate([zeros, w_a2], axis=0).astype(BF16)
    e = (jnp.arange(d)[:, None] // RWKV_HEAD == jnp.arange(LANES)[None, :]).astype(BF16)
    et = e.T
    vec = lambda z: z.reshape(1, -1)
    r, k, v, g, kk, bv, gate = _rwkv_prep(h, vec(ng), w_in.astype(BF16), vec(mu), vec(w0), ww_pad, vec(a0), wa_pad,
                                          vec(k_k), vec(k_a), e, et, tm=256)
    y = _rwkv_scan(r, k, v, g, kk, bv, tm=512)
    del nh
    return _rwkv_out(h, y, r, k, v, gate, vec(r_k), vec(gn_g), vec(gn_b), w_out.astype(BF16), e, et, tm=512)


def kernel(x, norm_g, final_g, rwkv_w_in, rwkv_mu, rwkv_w0, rwkv_w_w2, rwkv_a0, rwkv_w_a2, rwkv_k_k, rwkv_k_a, rwkv_r_k, rwkv_gn_g, rwkv_gn_b, rwkv_w_out, hgrn_lb_logits, hgrn_w_in, hgrn_gn_g, hgrn_w_out, conv_w_in, conv_w, conv_w_out, gmlp_w_in, gmlp_v_g, gmlp_w_s, gmlp_b_s, gmlp_w_out):
    depth = norm_g.shape[0]
    assert depth == 4, "the fused final norm assumes the gMLP layer is the last one"
    vec = lambda z: z.reshape(1, -1)
    h = x
    h = _rwkv_layer(h, norm_g[0], rwkv_w_in[0], rwkv_mu[0], rwkv_w0[0], rwkv_w_w2[0], rwkv_a0[0], rwkv_w_a2[0],
                    rwkv_k_k[0], rwkv_k_a[0], rwkv_r_k[0], rwkv_gn_g[0], rwkv_gn_b[0], rwkv_w_out[0])
    h = _hgrn_layer(h, vec(norm_g[1]), hgrn_w_in[0].astype(BF16), hgrn_lb_logits, vec(hgrn_gn_g[0]),
                    hgrn_w_out[0].astype(BF16), layer=1, tm=256)
    h = _conv_layer(h, vec(norm_g[2]), conv_w_in[0].astype(BF16), conv_w[0], conv_w_out[0].astype(BF16), tm=256)
    bs_full = jnp.repeat(gmlp_b_s[0].T, GMLP_GROUP, axis=1)
    h = _gmlp_layer(h, vec(norm_g[3]), gmlp_w_in[0].astype(BF16), vec(gmlp_v_g[0]), gmlp_w_s[0].astype(BF16),
                    bs_full, gmlp_w_out[0].astype(BF16), vec(final_g), tm=256)
    return h
```

```python
import functools

import jax
import jax.numpy as jnp
from jax import lax
from jax.experimental import pallas as pl
from jax.experimental.pallas import tpu as pltpu

F32 = jnp.float32
BF16 = jnp.bfloat16

RMS_EPS = 1e-6
RWKV_HEAD = 64
RWKV_GN_EPS = 64e-5
RWKV_LORA = 64
HGRN_HEAD = 128
GMLP_CHUNK = 128
GMLP_GROUP = 128
CONV_WIDTH = 3

LANES = 128
SUBLANES = 8
CHUNK = 64
HALF = CHUNK // 2
VMEM_LIMIT = 56 * 1024 * 1024


def _mm(a, b):
    return jnp.dot(a.astype(BF16), b.astype(BF16), preferred_element_type=F32)


def _mm_nt(a, b):
    return lax.dot_general(a.astype(BF16), b.astype(BF16), (((1,), (1,)), ((), ())),
                           preferred_element_type=F32)


def _mm_tn(a, b):
    return lax.dot_general(a.astype(BF16), b.astype(BF16), (((0,), (0,)), ((), ())),
                           preferred_element_type=F32)


def _split2(x):
    hi = x.astype(BF16)
    lo = (x - hi.astype(F32)).astype(BF16)
    return hi, lo


def _split3(x):
    hi = x.astype(BF16)
    r1 = x - hi.astype(F32)
    mid = r1.astype(BF16)
    lo = (r1 - mid.astype(F32)).astype(BF16)
    return hi, mid, lo


def _dot_exact_rhs(x, e):
    hi, lo = _split2(x)
    return (jnp.dot(hi, e, preferred_element_type=F32) + jnp.dot(lo, e, preferred_element_type=F32))


def _cumsum_rows(tri, g):
    hi, mid, lo = _split3(g)
    return (jnp.dot(tri, hi, preferred_element_type=F32) + jnp.dot(tri, mid, preferred_element_type=F32)
            + jnp.dot(tri, lo, preferred_element_type=F32))


def _cumsum_rows_fused(tri3, g):
    return jnp.dot(tri3, jnp.concatenate(_split3(g), axis=0), preferred_element_type=F32)


def _sigmoid(x):
    return 1.0 / (1.0 + jnp.exp(-x))


def _silu(x):
    return x * _sigmoid(x)


def _softplus(x):
    return jnp.maximum(x, 0.0) + jnp.log1p(jnp.exp(-jnp.abs(x)))


def _rmsnorm(x, g):
    ms = jnp.mean(x * x, axis=-1, keepdims=True)
    return x * lax.rsqrt(ms + RMS_EPS) * g


def _iota2(shape, axis):
    return lax.broadcasted_iota(jnp.int32, shape, axis)


def _tri_ones(n):
    return (_iota2((n, n), 0) >= _iota2((n, n), 1)).astype(BF16)


def _row_spec(tm, d):
    return pl.BlockSpec((None, tm, d), lambda b, t: (b, t, 0))


def _full_spec(shape):
    n = len(shape)
    return pl.BlockSpec(shape, lambda *_: (0,) * n)


def _params(n_axes):
    return pltpu.CompilerParams(dimension_semantics=("arbitrary",) * n_axes, vmem_limit_bytes=VMEM_LIMIT)


def _rwkv_prep_body(h_ref, ng_ref, win_ref, mu_ref, w0_ref, ww_ref, a0_ref, wa_ref, kk_ref, ka_ref,
                    e_ref, et_ref,
                    r_out, k_out, v_out, g_out, kk_out, bv_out, gate_out, carry_ref, *, d):
    tm = h_ref.shape[0]

    @pl.when(pl.program_id(1) == 0)
    def _():
        carry_ref[...] = jnp.zeros_like(carry_ref)

    hn = _rmsnorm(h_ref[...], ng_ref[...])
    p = _mm(hn, win_ref[...])
    row = _iota2(p.shape, 0)
    prev = jnp.where(row == 0, carry_ref[SUBLANES - 1:SUBLANES, :], pltpu.roll(p, 1, 0))
    carry_ref[...] = p[tm - SUBLANES:tm, :]
    p = p + (prev - p) * mu_ref[...]

    r = p[:, 0:d]
    k = p[:, d:2 * d]
    v = p[:, 2 * d:3 * d]
    lw = p[:, 3 * d:3 * d + 2 * RWKV_LORA]
    gate = p[:, 3 * d + 2 * RWKV_LORA:]
    lw = jnp.where(_iota2(lw.shape, 1) < RWKV_LORA, jnp.tanh(lw), lw)
    w_log = -_softplus(-(w0_ref[...] + _mm(lw, ww_ref[...]))) - 0.5
    a = _sigmoid(a0_ref[...] + _mm(lw, wa_ref[...]))
    kk = k * kk_ref[...]
    n2 = _dot_exact_rhs(_dot_exact_rhs(kk * kk, e_ref[...]), et_ref[...])
    kk = kk / jnp.maximum(jnp.sqrt(n2), 1e-12)
    r_out[...] = r
    k_out[...] = k * (1.0 + (a - 1.0) * ka_ref[...])
    v_out[...] = v
    g_out[...] = -jnp.exp(w_log)
    kk_out[...] = kk
    bv_out[...] = kk * a
    gate_out[...] = gate


def _rwkv_prep(h, ng, w_in, mu, w0, ww_pad, a0, wa_pad, k_k, k_a, e, et, *, tm):
    bsz, t, d = h.shape
    cols = w_in.shape[1]
    row = _row_spec(tm, d)
    vec = _full_spec((1, d))
    out = jax.ShapeDtypeStruct((bsz, t, d), F32)
    return pl.pallas_call(
        functools.partial(_rwkv_prep_body, d=d),
        grid=(bsz, t // tm),
        in_specs=[row, vec, _full_spec(w_in.shape), _full_spec((1, cols)), vec, _full_spec(ww_pad.shape), vec,
                  _full_spec(wa_pad.shape), vec, vec, _full_spec(e.shape), _full_spec(et.shape)],
        out_specs=[row] * 7,
        out_shape=[out] * 7,
        scratch_shapes=[pltpu.VMEM((SUBLANES, cols), F32)],
        compiler_params=_params(2),
        name="rwkv_prep",
    )(h, ng, w_in, mu, w0, ww_pad, a0, wa_pad, k_k, k_a, e, et)


def _stack_heads(x, m0):
    return jnp.concatenate([jnp.where(m0, x, 0.0), jnp.where(m0, 0.0, x)], axis=0)


def _unit_lower_inverse(lows, i2, j2):
    s = 1
    eye = jnp.where(i2 == j2, 1.0, 0.0)
    t_inv = [eye for _ in lows]
    while s < CHUNK:
        keep = ((i2 // (2 * s)) == (j2 // (2 * s))) & ((i2 // s) != (j2 // s))
        subs = [jnp.where(keep, low, 0.0) for low in lows]
        if s == 1:
            t_inv = [t + sub for t, sub in zip(t_inv, subs)]
        else:
            left = [_mm(t, sub) for t, sub in zip(t_inv, subs)]
            t_inv = [t + _mm(lt, t) for t, lt in zip(t_inv, left)]
        s *= 2
    return t_inv


def _rwkv_scan_body(r_ref, k_ref, v_ref, g_ref, kk_ref, bv_ref, y_ref, state_ref):
    tm = r_ref.shape[0]
    c = CHUNK
    w = 2 * RWKV_HEAD
    nc = tm // c
    n_pairs = r_ref.shape[1] // w

    @pl.when(pl.program_id(2) == 0)
    def _():
        state_ref[...] = jnp.zeros_like(state_ref)

    m0 = _iota2((c, w), 1) < RWKV_HEAD
    m0b = m0
    tri3 = jnp.concatenate([_tri_ones(c)] * 3, axis=1)
    i2 = _iota2((2 * c, w), 0)
    j2 = _iota2((2 * c, w), 1)
    same_head = (i2 // RWKV_HEAD) == (j2 // RWKV_HEAD)
    ir = _iota2((c, w), 0)
    jr = _iota2((c, w), 1) % c
    ir2 = _iota2((c, 2 * w), 0)
    jr2 = _iota2((c, 2 * w), 1) % c
    stack = functools.partial(_stack_heads, m0=m0)
    each = lambda f, *lists: [f(*xs) for xs in zip(*lists)]
    where = [(pl.ds(ci * c, c), pl.ds(pi * w, w)) for ci in range(nc) for pi in range(n_pairs)]
    load = lambda ref: [ref[rows, cols] for rows, cols in where]

    r, k, v, g, kk, bv = load(r_ref), load(k_ref), load(v_ref), load(g_ref), load(kk_ref), load(bv_ref)
    bc = each(lambda x: _cumsum_rows_fused(tri3, x), g)
    bl = each(lambda x: x[c - 1:c, :], bc)
    p_inv = each(lambda x: jnp.exp(-x), bc)
    p_end = each(lambda x, y: jnp.exp(x - y), bl, bc)
    rt = each(lambda a, x: a * jnp.exp(x), r, bc)
    at2 = each(lambda a, x, y: stack(-a * jnp.exp(x - y)).astype(BF16), kk, bc, g)
    v2 = each(lambda a: stack(a).astype(BF16), v)
    lhs = each(lambda a, x: jnp.concatenate([a, x.astype(BF16)], axis=0), at2, rt)
    rhs = each(lambda a, x, p: jnp.concatenate([stack(a * p), stack(x * p)], axis=0).astype(BF16), bv, k, p_inv)
    end = each(lambda a, x, p: jnp.concatenate([a * p, x * p], axis=0).astype(BF16), bv, k, p_end)

    prod = each(_mm_nt, lhs, rhs)
    low = each(lambda x: jnp.where(j2 < i2, x[0:2 * c, 0:w], 0.0), prod)
    a_ak = each(lambda x: jnp.where(jr < ir, x[0:c, w:2 * w] + x[c:2 * c, w:2 * w], 0.0).astype(BF16), prod)
    a_r = each(lambda x: jnp.where(jr2 <= ir2, x[2 * c:3 * c, :], 0.0).astype(BF16), prod)
    t_inv = _unit_lower_inverse(low, i2, j2)
    t_row = each(lambda t: (t[0:c, :] + t[c:2 * c, :]).astype(BF16), t_inv)
    z = each(_mm, a_ak, v2)
    w12 = each(lambda t, a, x: _mm(t, jnp.concatenate([a, stack(x).astype(BF16)], axis=1)), t_row, at2, z)
    w12b = each(lambda x: x.astype(BF16), w12)
    zero_v = jnp.zeros((2 * c, w), BF16)
    q12 = each(lambda a, x, y: _mm(a, jnp.concatenate(
        [jnp.concatenate([_stack_heads(x[:, 0:w], m0b), _stack_heads(x[:, w:2 * w], m0b)], axis=1),
         jnp.concatenate([zero_v, y], axis=1)], axis=0)), a_r, w12b, v2)
    q1 = each(lambda a, x: a + x[:, 0:w], rt, q12)
    q2 = each(lambda x: x[:, w:2 * w], q12)
    zero_c = jnp.zeros((c, w), BF16)
    ti = each(lambda e, x, y: _mm_tn(e, jnp.concatenate(
        [x, jnp.concatenate([zero_c, y.astype(BF16)], axis=1)], axis=0)), end, w12b, v)
    trans = each(lambda x, e: jnp.where(same_head, x[:, 0:w], 0.0)
                 + jnp.where(i2 == j2, jnp.broadcast_to(jnp.exp(e), (2 * c, w)), 0.0), ti, bl)
    inject = each(lambda x: jnp.where(same_head, x[:, w:2 * w], 0.0), ti)

    h_state = [state_ref[pi] for pi in range(n_pairs)]
    for n, (rows, cols) in enumerate(where):
        pi = n % n_pairs
        y_ref[rows, cols] = _mm(q1[n], h_state[pi]) + q2[n]
        h_state[pi] = _mm(trans[n], h_state[pi]) + inject[n]
    for pi in range(n_pairs):
        state_ref[pi] = h_state[pi]


def _rwkv_scan(r, k, v, g, kk, bv, *, tm, n_pairs):
    bsz, t, d = r.shape
    w = 2 * RWKV_HEAD
    spec = pl.BlockSpec((None, tm, w * n_pairs), lambda b, p, i: (b, i, p))
    return pl.pallas_call(
        _rwkv_scan_body,
        grid=(bsz, d // (w * n_pairs), t // tm),
        in_specs=[spec] * 6,
        out_specs=spec,
        out_shape=jax.ShapeDtypeStruct((bsz, t, d), F32),
        scratch_shapes=[pltpu.VMEM((n_pairs, w, w), F32)],
        compiler_params=_params(3),
        name="rwkv_scan",
    )(r, k, v, g, kk, bv)


def _rwkv_out_body(h_ref, y_ref, r_ref, k_ref, v_ref, gate_ref, rk_ref, gg_ref, gb_ref, wout_ref,
                   e_ref, et_ref, o_ref):
    seg = lambda z: _dot_exact_rhs(_dot_exact_rhs(z, e_ref[...]), et_ref[...])
    y = y_ref[...]
    mean = seg(y) * (1.0 / RWKV_HEAD)
    dy = y - mean
    var = seg(dy * dy) * (1.0 / RWKV_HEAD)
    yn = dy * lax.rsqrt(var + RWKV_GN_EPS) * gg_ref[...] + gb_ref[...]
    bonus = seg(r_ref[...] * k_ref[...] * rk_ref[...]) * v_ref[...]
    out = (yn + bonus) * _silu(gate_ref[...])
    o_ref[...] = h_ref[...] + _mm(out, wout_ref[...])


def _rwkv_out(h, y, r, k, v, gate, r_k, gn_g, gn_b, w_out, e, et, *, tm):
    bsz, t, d = h.shape
    row = _row_spec(tm, d)
    vec = _full_spec((1, d))
    return pl.pallas_call(
        _rwkv_out_body,
        grid=(bsz, t // tm),
        in_specs=[row] * 6 + [vec, vec, vec, _full_spec(w_out.shape), _full_spec(e.shape), _full_spec(et.shape)],
        out_specs=row,
        out_shape=jax.ShapeDtypeStruct((bsz, t, d), F32),
        compiler_params=_params(2),
        name="rwkv_out",
    )(h, y, r, k, v, gate, r_k, gn_g, gn_b, w_out, e, et)


def _hgrn_body(h_ref, ng_ref, win_ref, lbl_ref, gg_ref, wout_ref, o_ref, state_ref, *, layer):
    tm, d = h_ref.shape
    c = CHUNK
    nh = d // HGRN_HEAD

    @pl.when(pl.program_id(1) == 0)
    def _():
        state_ref[...] = jnp.zeros_like(state_ref)

    x = h_ref[...]
    p = _mm(_rmsnorm(x, ng_ref[...]), win_ref[...])
    q = p[:, 0:d]
    f_pre = p[:, d:2 * d]
    v = p[:, 2 * d:3 * d]
    gate = p[:, 3 * d:4 * d]

    logits = lbl_ref[...]
    ex = jnp.exp(logits - jnp.max(logits, axis=0, keepdims=True))
    lb = jnp.sum(ex[1:layer + 1, :], axis=0, keepdims=True) / jnp.sum(ex, axis=0, keepdims=True)
    log_sig = jnp.minimum(f_pre, 0.0) - jnp.log1p(jnp.exp(-jnp.abs(f_pre)))
    t1 = jnp.log(lb)
    t2 = jnp.log1p(-lb) + log_sig
    log_f = jnp.maximum(t1, t2) + jnp.log1p(jnp.exp(-jnp.abs(t1 - t2)))
    k = (1.0 - lb) * _sigmoid(-f_pre)

    nc = tm // c
    tri3 = jnp.concatenate([_tri_ones(c)] * 3, axis=1)
    row = _iota2((c, HGRN_HEAD), 0)
    top = row < HALF
    ri, ci_ = _iota2((c, c), 0), _iota2((c, c), 1)
    diag_blocks = (ri // HALF) == (ci_ // HALF)
    causal = ri >= ci_
    each = lambda f, *lists: [f(*xs) for xs in zip(*lists)]
    where = [(slice(ci * c, (ci + 1) * c), slice(hi * HGRN_HEAD, (hi + 1) * HGRN_HEAD))
             for ci in range(nc) for hi in range(nh)]
    take = lambda z: [z[rows, cols] for rows, cols in where]
    qs, ks, vs, gs = take(q), take(k), take(v), take(log_f)
    b = each(lambda x: _cumsum_rows_fused(tri3, x), gs)
    bl = each(lambda x: x[c - 1:c], b)
    beta = each(lambda x: x[HALF - 1:HALF], b)
    mid = each(lambda x: jnp.where(top, x[HALF // 2:HALF // 2 + 1], x[HALF + HALF // 2:HALF + HALF // 2 + 1]), b)
    q_d = each(lambda a, x, m: (a * jnp.exp(x - m)).astype(BF16), qs, b, mid)
    k_d = each(lambda a, x, m: (a * jnp.exp(m - x)).astype(BF16), ks, b, mid)
    q_o = each(lambda a, x, e: jnp.where(top, 0.0, a * jnp.exp(jnp.minimum(x - e, 0.0))).astype(BF16), qs, b, beta)
    k_o = each(lambda a, x, e: jnp.where(top, a * jnp.exp(jnp.minimum(e - x, 0.0)), 0.0).astype(BF16), ks, b, beta)
    q_in = each(lambda a, x: (a * jnp.exp(x)).astype(BF16), qs, b)
    k_out = each(lambda a, x, e: (a * jnp.exp(e - x)).astype(BF16), ks, b, bl)
    v_b = each(lambda a: a.astype(BF16), vs)
    att = each(lambda qd, kd, qo, ko: jnp.where(diag_blocks, jnp.where(causal, _mm_nt(qd, kd), 0.0), _mm_nt(qo, ko)),
               q_d, k_d, q_o, k_o)
    o_intra = each(_mm, att, v_b)
    kv = each(_mm_tn, v_b, k_out)
    s_t = [state_ref[hi] for hi in range(nh)]
    s_in = []
    for n in range(len(where)):
        hi = n % nh
        s_in.append(s_t[hi])
        s_t[hi] = s_t[hi] * jnp.exp(bl[n]) + kv[n]
    for hi in range(nh):
        state_ref[hi] = s_t[hi]
    o = each(lambda oi, a, s: oi + _mm_nt(a, s), o_intra, q_in, s_in)
    o = each(lambda z: z * lax.rsqrt(jnp.mean(z * z, axis=-1, keepdims=True) + RMS_EPS), o)
    o_all = jnp.concatenate([jnp.concatenate(o[ci * nh:(ci + 1) * nh], axis=1) for ci in range(nc)], axis=0)
    o_all = o_all * gg_ref[...] * _silu(gate)
    o_ref[...] = x + _mm(o_all, wout_ref[...])


def _hgrn_layer(h, ng, w_in, lb_logits, gn_g, w_out, *, layer, tm):
    bsz, t, d = h.shape
    row = _row_spec(tm, d)
    vec = _full_spec((1, d))
    return pl.pallas_call(
        functools.partial(_hgrn_body, layer=layer),
        grid=(bsz, t // tm),
        in_specs=[row, vec, _full_spec(w_in.shape), _full_spec(lb_logits.shape), vec, _full_spec(w_out.shape)],
        out_specs=row,
        out_shape=jax.ShapeDtypeStruct((bsz, t, d), F32),
        scratch_shapes=[pltpu.VMEM((d // HGRN_HEAD, HGRN_HEAD, HGRN_HEAD), F32)],
        compiler_params=_params(2),
        name="hgrn_layer",
    )(h, ng, w_in, lb_logits, gn_g, w_out)


def _conv_body(h_ref, ng_ref, win_ref, cw_ref, wout_ref, o_ref, carry_ref):
    tm, d = h_ref.shape

    @pl.when(pl.program_id(1) == 0)
    def _():
        carry_ref[...] = jnp.zeros_like(carry_ref)

    x = h_ref[...]
    p = _mm(_rmsnorm(x, ng_ref[...]), win_ref[...])
    b_gate = p[:, 0:d]
    y = p[:, d:2 * d] * p[:, 2 * d:3 * d]
    gate = p[:, 3 * d:4 * d]
    row = _iota2((tm, d), 0)
    last = carry_ref[SUBLANES - 1:SUBLANES, :]
    last2 = carry_ref[SUBLANES - 2:SUBLANES - 1, :]
    y1 = jnp.where(row == 0, last, pltpu.roll(y, 1, 0))
    y2 = jnp.where(row == 0, last2, jnp.where(row == 1, last, pltpu.roll(y, 2, 0)))
    carry_ref[...] = y[tm - SUBLANES:tm, :]
    yc = cw_ref[0:1, :] * y2 + cw_ref[1:2, :] * y1 + cw_ref[2:3, :] * y
    o_ref[...] = x + _mm(b_gate * yc * _silu(gate), wout_ref[...])


def _conv_layer(h, ng, w_in, conv_w, w_out, *, tm):
    bsz, t, d = h.shape
    row = _row_spec(tm, d)
    return pl.pallas_call(
        _conv_body,
        grid=(bsz, t // tm),
        in_specs=[row, _full_spec((1, d)), _full_spec(w_in.shape), _full_spec(conv_w.shape), _full_spec(w_out.shape)],
        out_specs=row,
        out_shape=jax.ShapeDtypeStruct((bsz, t, d), F32),
        scratch_shapes=[pltpu.VMEM((SUBLANES, d), F32)],
        compiler_params=_params(2),
        name="conv_layer",
    )(h, ng, w_in, conv_w, w_out)


def _gmlp_body(h_ref, ng_ref, win_ref, vg_ref, ws_ref, bs_ref, wout_ref, fg_ref, o_ref):
    tm, d = h_ref.shape
    c = GMLP_CHUNK
    x = h_ref[...]
    p = _mm(_rmsnorm(x, ng_ref[...]), win_ref[...])
    u = p[:, 0:d]
    v = _rmsnorm(p[:, d:2 * d], vg_ref[...])
    gate = p[:, 2 * d:3 * d]
    causal = _iota2((c, c), 0) >= _iota2((c, c), 1)
    bias = bs_ref[...]
    chunk_out = []
    for ci in range(tm // c):
        rows = slice(ci * c, (ci + 1) * c)
        parts = []
        for gi in range(d // GMLP_GROUP):
            cols = slice(gi * GMLP_GROUP, (gi + 1) * GMLP_GROUP)
            parts.append(_mm(jnp.where(causal, ws_ref[gi], 0.0), v[rows, cols]))
        chunk_out.append(jnp.concatenate(parts, axis=1) + bias)
    s = jnp.concatenate(chunk_out, axis=0)
    out = x + _mm(u * s * _silu(gate), wout_ref[...])
    o_ref[...] = _rmsnorm(out, fg_ref[...])


def _gmlp_layer(h, ng, w_in, v_g, w_s, bs_full, w_out, final_g, *, tm):
    bsz, t, d = h.shape
    row = _row_spec(tm, d)
    vec = _full_spec((1, d))
    return pl.pallas_call(
        _gmlp_body,
        grid=(bsz, t // tm),
        in_specs=[row, vec, _full_spec(w_in.shape), vec, _full_spec(w_s.shape), _full_spec(bs_full.shape),
                  _full_spec(w_out.shape), vec],
        out_specs=row,
        out_shape=jax.ShapeDtypeStruct((bsz, t, d), F32),
        compiler_params=_params(2),
        name="gmlp_layer",
    )(h, ng, w_in, v_g, w_s, bs_full, w_out, final_g)


def _rwkv_layer(h, ng, w_in, mu, w0, w_w2, a0, w_a2, k_k, k_a, r_k, gn_g, gn_b, w_out):
    d = h.shape[-1]
    nh = d // RWKV_HEAD
    zeros = jnp.zeros((RWKV_LORA, d), F32)
    ww_pad = jnp.concatenate([w_w2, zeros], axis=0).astype(BF16)
    wa_pad = jnp.concatenate([zeros, w_a2], axis=0).astype(BF16)
    e = (jnp.arange(d)[:, None] // RWKV_HEAD == jnp.arange(LANES)[None, :]).astype(BF16)
    et = e.T
    vec = lambda z: z.reshape(1, -1)
    r, k, v, g, kk, bv, gate = _rwkv_prep(h, vec(ng), w_in.astype(BF16), vec(mu), vec(w0), ww_pad, vec(a0), wa_pad,
                                          vec(k_k), vec(k_a), e, et, tm=256)
    y = _rwkv_scan(r, k, v, g, kk, bv, tm=128, n_pairs=8)
    del nh
    return _rwkv_out(h, y, r, k, v, gate, vec(r_k), vec(gn_g), vec(gn_b), w_out.astype(BF16), e, et, tm=512)


def kernel(x, norm_g, final_g, rwkv_w_in, rwkv_mu, rwkv_w0, rwkv_w_w2, rwkv_a0, rwkv_w_a2, rwkv_k_k, rwkv_k_a, rwkv_r_k, rwkv_gn_g, rwkv_gn_b, rwkv_w_out, hgrn_lb_logits, hgrn_w_in, hgrn_gn_g, hgrn_w_out, conv_w_in, conv_w, conv_w_out, gmlp_w_in, gmlp_v_g, gmlp_w_s, gmlp_b_s, gmlp_w_out):
    depth = norm_g.shape[0]
    assert depth == 4, "the fused final norm assumes the gMLP layer is the last one"
    vec = lambda z: z.reshape(1, -1)
    h = x
    h = _rwkv_layer(h, norm_g[0], rwkv_w_in[0], rwkv_mu[0], rwkv_w0[0], rwkv_w_w2[0], rwkv_a0[0], rwkv_w_a2[0],
                    rwkv_k_k[0], rwkv_k_a[0], rwkv_r_k[0], rwkv_gn_g[0], rwkv_gn_b[0], rwkv_w_out[0])
    h = _hgrn_layer(h, vec(norm_g[1]), hgrn_w_in[0].astype(BF16), hgrn_lb_logits, vec(hgrn_gn_g[0]),
                    hgrn_w_out[0].astype(BF16), layer=1, tm=256)
    h = _conv_layer(h, vec(norm_g[2]), conv_w_in[0].astype(BF16), conv_w[0], conv_w_out[0].astype(BF16), tm=256)
    bs_full = jnp.repeat(gmlp_b_s[0].T, GMLP_GROUP, axis=1)
    h = _gmlp_layer(h, vec(norm_g[3]), gmlp_w_in[0].astype(BF16), vec(gmlp_v_g[0]), gmlp_w_s[0].astype(BF16),
                    bs_full, gmlp_w_out[0].astype(BF16), vec(final_g), tm=256)
    return h
```

```python
import functools

import jax
import jax.numpy as jnp
from jax import lax
from jax.experimental import pallas as pl
from jax.experimental.pallas import tpu as pltpu

F32 = jnp.float32
BF16 = jnp.bfloat16

RMS_EPS = 1e-6
RWKV_HEAD = 64
RWKV_GN_EPS = 64e-5
RWKV_LORA = 64
HGRN_HEAD = 128
GMLP_CHUNK = 128
GMLP_GROUP = 128
CONV_WIDTH = 3

LANES = 128
SUBLANES = 8
CHUNK = 64
HALF = CHUNK // 2
VMEM_LIMIT = 56 * 1024 * 1024


def _mm(a, b):
    return jnp.dot(a.astype(BF16), b.astype(BF16), preferred_element_type=F32)


def _mm_nt(a, b):
    return lax.dot_general(a.astype(BF16), b.astype(BF16), (((1,), (1,)), ((), ())),
                           preferred_element_type=F32)


def _mm_tn(a, b):
    return lax.dot_general(a.astype(BF16), b.astype(BF16), (((0,), (0,)), ((), ())),
                           preferred_element_type=F32)


def _split2(x):
    hi = x.astype(BF16)
    lo = (x - hi.astype(F32)).astype(BF16)
    return hi, lo


def _split3(x):
    hi = x.astype(BF16)
    r1 = x - hi.astype(F32)
    mid = r1.astype(BF16)
    lo = (r1 - mid.astype(F32)).astype(BF16)
    return hi, mid, lo


def _head_sums(x, e, et2):
    hi, lo = _split2(jnp.dot(x.astype(BF16), e, preferred_element_type=F32))
    return jnp.dot(jnp.concatenate([hi, lo], axis=1), et2, preferred_element_type=F32)


def _cumsum_rows_fused(tri3, g):
    return jnp.dot(tri3, jnp.concatenate(_split3(g), axis=0), preferred_element_type=F32)


def _sigmoid(x):
    return 0.5 * jnp.tanh(0.5 * x) + 0.5


def _silu(x):
    return x * _sigmoid(x)


def _log1pexp_neg(z):
    return jnp.log(1.0 + jnp.exp(-z))


def _softplus(x):
    return jnp.maximum(x, 0.0) + _log1pexp_neg(jnp.abs(x))


def _rmsnorm(x, g):
    ms = jnp.mean(x * x, axis=-1, keepdims=True)
    return x * lax.rsqrt(ms + RMS_EPS) * g


def _iota2(shape, axis):
    return lax.broadcasted_iota(jnp.int32, shape, axis)


def _tri_ones(n):
    return (_iota2((n, n), 0) >= _iota2((n, n), 1)).astype(BF16)


def _row_spec(tm, d):
    return pl.BlockSpec((None, tm, d), lambda b, t: (b, t, 0))


def _full_spec(shape):
    n = len(shape)
    return pl.BlockSpec(shape, lambda *_: (0,) * n)


def _params(n_axes):
    return pltpu.CompilerParams(dimension_semantics=("arbitrary",) * n_axes, vmem_limit_bytes=VMEM_LIMIT)


def _rwkv_prep_body(h_ref, ng_ref, win_ref, mu_ref, w0_ref, ww_ref, a0_ref, wa_ref, kk_ref, ka_ref,
                    e_ref, et_ref,
                    r_out, k_out, v_out, g_out, kk_out, bv_out, gate_out, carry_ref, *, d):
    tm = h_ref.shape[0]

    @pl.when(pl.program_id(1) == 0)
    def _():
        carry_ref[...] = jnp.zeros_like(carry_ref)

    hn = _rmsnorm(h_ref[...], ng_ref[...])
    p = _mm(hn, win_ref[...])
    row = _iota2(p.shape, 0)
    prev = jnp.where(row == 0, carry_ref[SUBLANES - 1:SUBLANES, :], pltpu.roll(p, 1, 0))
    carry_ref[...] = p[tm - SUBLANES:tm, :]
    p = p + (prev - p) * mu_ref[...]

    r = p[:, 0:d]
    k = p[:, d:2 * d]
    v = p[:, 2 * d:3 * d]
    lw = p[:, 3 * d:3 * d + 2 * RWKV_LORA]
    gate = p[:, 3 * d + 2 * RWKV_LORA:]
    lw = jnp.where(_iota2(lw.shape, 1) < RWKV_LORA, jnp.tanh(lw), lw)
    w_log = -_softplus(-(w0_ref[...] + _mm(lw, ww_ref[...]))) - 0.5
    a = _sigmoid(a0_ref[...] + _mm(lw, wa_ref[...]))
    kk = k * kk_ref[...]
    n2 = _head_sums(kk * kk, e_ref[...], et_ref[...])
    kk = kk * lax.rsqrt(jnp.maximum(n2, 1e-24))
    r_out[...] = r
    k_out[...] = k * (1.0 + (a - 1.0) * ka_ref[...])
    v_out[...] = v
    g_out[...] = -jnp.exp(w_log)
    kk_out[...] = kk
    bv_out[...] = kk * a
    gate_out[...] = gate


def _rwkv_prep(h, ng, w_in, mu, w0, ww_pad, a0, wa_pad, k_k, k_a, e, et, *, tm):
    bsz, t, d = h.shape
    cols = w_in.shape[1]
    row = _row_spec(tm, d)
    vec = _full_spec((1, d))
    out = jax.ShapeDtypeStruct((bsz, t, d), F32)
    return pl.pallas_call(
        functools.partial(_rwkv_prep_body, d=d),
        grid=(bsz, t // tm),
        in_specs=[row, vec, _full_spec(w_in.shape), _full_spec((1, cols)), vec, _full_spec(ww_pad.shape), vec,
                  _full_spec(wa_pad.shape), vec, vec, _full_spec(e.shape), _full_spec(et.shape)],
        out_specs=[row] * 7,
        out_shape=[out] * 7,
        scratch_shapes=[pltpu.VMEM((SUBLANES, cols), F32)],
        compiler_params=_params(2),
        name="rwkv_prep",
    )(h, ng, w_in, mu, w0, ww_pad, a0, wa_pad, k_k, k_a, e, et)


def _stack_heads(x, m0):
    return jnp.concatenate([jnp.where(m0, x, 0.0), jnp.where(m0, 0.0, x)], axis=0)


def _unit_lower_inverse(lows, i2, j2):
    s = 1
    eye = jnp.where(i2 == j2, 1.0, 0.0)
    t_inv = [eye for _ in lows]
    while s < CHUNK:
        keep = ((i2 // (2 * s)) == (j2 // (2 * s))) & ((i2 // s) != (j2 // s))
        subs = [jnp.where(keep, low, 0.0) for low in lows]
        if s == 1:
            t_inv = [t + sub for t, sub in zip(t_inv, subs)]
        else:
            left = [_mm(t, sub) for t, sub in zip(t_inv, subs)]
            t_inv = [t + _mm(lt, t) for t, lt in zip(t_inv, left)]
        s *= 2
    return t_inv


def _rwkv_scan_body(r_ref, k_ref, v_ref, g_ref, kk_ref, bv_ref, y_ref, state_ref):
    tm = r_ref.shape[0]
    c = CHUNK
    w = 2 * RWKV_HEAD
    nc = tm // c
    n_pairs = r_ref.shape[1] // w

    @pl.when(pl.program_id(2) == 0)
    def _():
        state_ref[...] = jnp.zeros_like(state_ref)

    m0 = _iota2((c, w), 1) < RWKV_HEAD
    m0b = m0
    tri3 = jnp.concatenate([_tri_ones(c)] * 3, axis=1)
    i2 = _iota2((2 * c, w), 0)
    j2 = _iota2((2 * c, w), 1)
    same_head = (i2 // RWKV_HEAD) == (j2 // RWKV_HEAD)
    ir = _iota2((c, w), 0)
    jr = _iota2((c, w), 1) % c
    ir2 = _iota2((c, 2 * w), 0)
    jr2 = _iota2((c, 2 * w), 1) % c
    stack = functools.partial(_stack_heads, m0=m0)
    each = lambda f, *lists: [f(*xs) for xs in zip(*lists)]
    where = [(pl.ds(ci * c, c), pl.ds(pi * w, w)) for ci in range(nc) for pi in range(n_pairs)]
    load = lambda ref: [ref[rows, cols] for rows, cols in where]

    r, k, v, g, kk, bv = load(r_ref), load(k_ref), load(v_ref), load(g_ref), load(kk_ref), load(bv_ref)
    bc = each(lambda x: _cumsum_rows_fused(tri3, x), g)
    bl = each(lambda x: x[c - 1:c, :], bc)
    p_inv = each(lambda x: jnp.exp(-x), bc)
    p_end = each(lambda x, y: jnp.exp(x - y), bl, bc)
    rt = each(lambda a, x: a * jnp.exp(x), r, bc)
    at2 = each(lambda a, x, y: stack(-a * jnp.exp(x - y)).astype(BF16), kk, bc, g)
    v2 = each(lambda a: stack(a).astype(BF16), v)
    lhs = each(lambda a, x: jnp.concatenate([a, x.astype(BF16)], axis=0), at2, rt)
    rhs = each(lambda a, x, p: jnp.concatenate([stack(a * p), stack(x * p)], axis=0).astype(BF16), bv, k, p_inv)
    end = each(lambda a, x, p: jnp.concatenate([a * p, x * p], axis=0).astype(BF16), bv, k, p_end)

    prod = each(_mm_nt, lhs, rhs)
    low = each(lambda x: jnp.where(j2 < i2, x[0:2 * c, 0:w], 0.0), prod)
    a_ak = each(lambda x: jnp.where(jr < ir, x[0:c, w:2 * w] + x[c:2 * c, w:2 * w], 0.0).astype(BF16), prod)
    a_r = each(lambda x: jnp.where(jr2 <= ir2, x[2 * c:3 * c, :], 0.0).astype(BF16), prod)
    t_inv = _unit_lower_inverse(low, i2, j2)
    t_row = each(lambda t: (t[0:c, :] + t[c:2 * c, :]).astype(BF16), t_inv)
    z = each(_mm, a_ak, v2)
    w12 = each(lambda t, a, x: _mm(t, jnp.concatenate([a, stack(x).astype(BF16)], axis=1)), t_row, at2, z)
    w12b = each(lambda x: x.astype(BF16), w12)
    zero_v = jnp.zeros((2 * c, w), BF16)
    q12 = each(lambda a, x, y: _mm(a, jnp.concatenate(
        [jnp.concatenate([_stack_heads(x[:, 0:w], m0b), _stack_heads(x[:, w:2 * w], m0b)], axis=1),
         jnp.concatenate([zero_v, y], axis=1)], axis=0)), a_r, w12b, v2)
    q1 = each(lambda a, x: a + x[:, 0:w], rt, q12)
    q2 = each(lambda x: x[:, w:2 * w], q12)
    zero_c = jnp.zeros((c, w), BF16)
    ti = each(lambda e, x, y: _mm_tn(e, jnp.concatenate(
        [x, jnp.concatenate([zero_c, y.astype(BF16)], axis=1)], axis=0)), end, w12b, v)
    trans = each(lambda x, e: jnp.where(same_head, x[:, 0:w], 0.0)
                 + jnp.where(i2 == j2, jnp.broadcast_to(jnp.exp(e), (2 * c, w)), 0.0), ti, bl)
    inject = each(lambda x: jnp.where(same_head, x[:, w:2 * w], 0.0), ti)

    h_state = [state_ref[pi] for pi in range(n_pairs)]
    for n, (rows, cols) in enumerate(where):
        pi = n % n_pairs
        y_ref[rows, cols] = _mm(q1[n], h_state[pi]) + q2[n]
        h_state[pi] = _mm(trans[n], h_state[pi]) + inject[n]
    for pi in range(n_pairs):
        state_ref[pi] = h_state[pi]


def _rwkv_scan(r, k, v, g, kk, bv, *, tm, n_pairs):
    bsz, t, d = r.shape
    w = 2 * RWKV_HEAD
    spec = pl.BlockSpec((None, tm, w * n_pairs), lambda b, p, i: (b, i, p))
    return pl.pallas_call(
        _rwkv_scan_body,
        grid=(bsz, d // (w * n_pairs), t // tm),
        in_specs=[spec] * 6,
        out_specs=spec,
        out_shape=jax.ShapeDtypeStruct((bsz, t, d), F32),
        scratch_shapes=[pltpu.VMEM((n_pairs, w, w), F32)],
        compiler_params=_params(3),
        name="rwkv_scan",
    )(r, k, v, g, kk, bv)


def _rwkv_out_body(h_ref, y_ref, r_ref, k_ref, v_ref, gate_ref, rk_ref, gg_ref, gb_ref, wout_ref,
                   e_ref, et_ref, o_ref):
    seg = lambda z: _head_sums(z, e_ref[...], et_ref[...])
    y = y_ref[...]
    mean = seg(y) * (1.0 / RWKV_HEAD)
    dy = y - mean
    var = seg(dy * dy) * (1.0 / RWKV_HEAD)
    yn = dy * lax.rsqrt(var + RWKV_GN_EPS) * gg_ref[...] + gb_ref[...]
    bonus = seg(r_ref[...] * k_ref[...] * rk_ref[...]) * v_ref[...]
    out = (yn + bonus) * _silu(gate_ref[...])
    o_ref[...] = h_ref[...] + _mm(out, wout_ref[...])


def _rwkv_out(h, y, r, k, v, gate, r_k, gn_g, gn_b, w_out, e, et, *, tm):
    bsz, t, d = h.shape
    row = _row_spec(tm, d)
    vec = _full_spec((1, d))
    return pl.pallas_call(
        _rwkv_out_body,
        grid=(bsz, t // tm),
        in_specs=[row] * 6 + [vec, vec, vec, _full_spec(w_out.shape), _full_spec(e.shape), _full_spec(et.shape)],
        out_specs=row,
        out_shape=jax.ShapeDtypeStruct((bsz, t, d), F32),
        compiler_params=_params(2),
        name="rwkv_out",
    )(h, y, r, k, v, gate, r_k, gn_g, gn_b, w_out, e, et)


def _hgrn_body(h_ref, ng_ref, win_ref, lbl_ref, gg_ref, wout_ref, o_ref, state_ref, *, layer):
    tm, d = h_ref.shape
    c = CHUNK
    nh = d // HGRN_HEAD

    @pl.when(pl.program_id(1) == 0)
    def _():
        state_ref[...] = jnp.zeros_like(state_ref)

    x = h_ref[...]
    p = _mm(_rmsnorm(x, ng_ref[...]), win_ref[...])
    q = p[:, 0:d]
    f_pre = p[:, d:2 * d]
    v = p[:, 2 * d:3 * d]
    gate = p[:, 3 * d:4 * d]

    logits = lbl_ref[...]
    ex = jnp.exp(logits - jnp.max(logits, axis=0, keepdims=True))
    lb = jnp.sum(ex[1:layer + 1, :], axis=0, keepdims=True) / jnp.sum(ex, axis=0, keepdims=True)
    log_sig = jnp.minimum(f_pre, 0.0) - _log1pexp_neg(jnp.abs(f_pre))
    t1 = jnp.log(lb)
    t2 = jnp.log(1.0 - lb) + log_sig
    log_f = jnp.maximum(t1, t2) + _log1pexp_neg(jnp.abs(t1 - t2))
    k = (1.0 - lb) * _sigmoid(-f_pre)

    nc = tm // c
    tri3 = jnp.concatenate([_tri_ones(c)] * 3, axis=1)
    row = _iota2((c, HGRN_HEAD), 0)
    top = row < HALF
    ri, ci_ = _iota2((c, c), 0), _iota2((c, c), 1)
    diag_blocks = (ri // HALF) == (ci_ // HALF)
    causal = ri >= ci_
    each = lambda f, *lists: [f(*xs) for xs in zip(*lists)]
    where = [(slice(ci * c, (ci + 1) * c), slice(hi * HGRN_HEAD, (hi + 1) * HGRN_HEAD))
             for ci in range(nc) for hi in range(nh)]
    take = lambda z: [z[rows, cols] for rows, cols in where]
    qs, ks, vs, gs = take(q), take(k), take(v), take(log_f)
    b = each(lambda x: _cumsum_rows_fused(tri3, x), gs)
    bl = each(lambda x: x[c - 1:c], b)
    beta = each(lambda x: x[HALF - 1:HALF], b)
    mid = each(lambda x: jnp.where(top, x[HALF // 2:HALF // 2 + 1], x[HALF + HALF // 2:HALF + HALF // 2 + 1]), b)
    q_d = each(lambda a, x, m: (a * jnp.exp(x - m)).astype(BF16), qs, b, mid)
    k_d = each(lambda a, x, m: (a * jnp.exp(m - x)).astype(BF16), ks, b, mid)
    q_o = each(lambda a, x, e: jnp.where(top, 0.0, a * jnp.exp(jnp.minimum(x - e, 0.0))).astype(BF16), qs, b, beta)
    k_o = each(lambda a, x, e: jnp.where(top, a * jnp.exp(jnp.minimum(e - x, 0.0)), 0.0).astype(BF16), ks, b, beta)
    q_in = each(lambda a, x: (a * jnp.exp(x)).astype(BF16), qs, b)
    k_out = each(lambda a, x, e: (a * jnp.exp(e - x)).astype(BF16), ks, b, bl)
    v_b = each(lambda a: a.astype(BF16), vs)
    att = each(lambda qd, kd, qo, ko: jnp.where(diag_blocks, jnp.where(causal, _mm_nt(qd, kd), 0.0), _mm_nt(qo, ko)),
               q_d, k_d, q_o, k_o)
    o_intra = each(_mm, att, v_b)
    kv = each(_mm_tn, v_b, k_out)
    s_t = [state_ref[hi] for hi in range(nh)]
    s_in = []
    for n in range(len(where)):
        hi = n % nh
        s_in.append(s_t[hi])
        s_t[hi] = s_t[hi] * jnp.exp(bl[n]) + kv[n]
    for hi in range(nh):
        state_ref[hi] = s_t[hi]
    o = each(lambda oi, a, s: oi + _mm_nt(a, s), o_intra, q_in, s_in)
    o = each(lambda z: z * lax.rsqrt(jnp.mean(z * z, axis=-1, keepdims=True) + RMS_EPS), o)
    o_all = jnp.concatenate([jnp.concatenate(o[ci * nh:(ci + 1) * nh], axis=1) for ci in range(nc)], axis=0)
    o_all = o_all * gg_ref[...] * _silu(gate)
    o_ref[...] = x + _mm(o_all, wout_ref[...])


def _hgrn_layer(h, ng, w_in, lb_logits, gn_g, w_out, *, layer, tm):
    bsz, t, d = h.shape
    row = _row_spec(tm, d)
    vec = _full_spec((1, d))
    return pl.pallas_call(
        functools.partial(_hgrn_body, layer=layer),
        grid=(bsz, t // tm),
        in_specs=[row, vec, _full_spec(w_in.shape), _full_spec(lb_logits.shape), vec, _full_spec(w_out.shape)],
        out_specs=row,
        out_shape=jax.ShapeDtypeStruct((bsz, t, d), F32),
        scratch_shapes=[pltpu.VMEM((d // HGRN_HEAD, HGRN_HEAD, HGRN_HEAD), F32)],
        compiler_params=_params(2),
        name="hgrn_layer",
    )(h, ng, w_in, lb_logits, gn_g, w_out)


def _conv_body(h_ref, ng_ref, win_ref, cw_ref, wout_ref, o_ref, carry_ref):
    tm, d = h_ref.shape

    @pl.when(pl.program_id(1) == 0)
    def _():
        carry_ref[...] = jnp.zeros_like(carry_ref)

    x = h_ref[...]
    p = _mm(_rmsnorm(x, ng_ref[...]), win_ref[...])
    b_gate = p[:, 0:d]
    y = p[:, d:2 * d] * p[:, 2 * d:3 * d]
    gate = p[:, 3 * d:4 * d]
    row = _iota2((tm, d), 0)
    last = carry_ref[SUBLANES - 1:SUBLANES, :]
    last2 = carry_ref[SUBLANES - 2:SUBLANES - 1, :]
    y1 = jnp.where(row == 0, last, pltpu.roll(y, 1, 0))
    y2 = jnp.where(row == 0, last2, jnp.where(row == 1, last, pltpu.roll(y, 2, 0)))
    carry_ref[...] = y[tm - SUBLANES:tm, :]
    yc = cw_ref[0:1, :] * y2 + cw_ref[1:2, :] * y1 + cw_ref[2:3, :] * y
    o_ref[...] = x + _mm(b_gate * yc * _silu(gate), wout_ref[...])


def _conv_layer(h, ng, w_in, conv_w, w_out, *, tm):
    bsz, t, d = h.shape
    row = _row_spec(tm, d)
    return pl.pallas_call(
        _conv_body,
        grid=(bsz, t // tm),
        in_specs=[row, _full_spec((1, d)), _full_spec(w_in.shape), _full_spec(conv_w.shape), _full_spec(w_out.shape)],
        out_specs=row,
        out_shape=jax.ShapeDtypeStruct((bsz, t, d), F32),
        scratch_shapes=[pltpu.VMEM((SUBLANES, d), F32)],
        compiler_params=_params(2),
        name="conv_layer",
    )(h, ng, w_in, conv_w, w_out)


def _gmlp_body(h_ref, ng_ref, win_ref, vg_ref, ws_ref, bs_ref, wout_ref, fg_ref, o_ref):
    tm, d = h_ref.shape
    c = GMLP_CHUNK
    x = h_ref[...]
    p = _mm(_rmsnorm(x, ng_ref[...]), win_ref[...])
    u = p[:, 0:d]
    v = _rmsnorm(p[:, d:2 * d], vg_ref[...])
    gate = p[:, 2 * d:3 * d]
    causal = _iota2((c, c), 0) >= _iota2((c, c), 1)
    bias = bs_ref[...]
    chunk_out = []
    for ci in range(tm // c):
        rows = slice(ci * c, (ci + 1) * c)
        parts = []
        for gi in range(d // GMLP_GROUP):
            cols = slice(gi * GMLP_GROUP, (gi + 1) * GMLP_GROUP)
            parts.append(_mm(jnp.where(causal, ws_ref[gi], 0.0), v[rows, cols]))
        chunk_out.append(jnp.concatenate(parts, axis=1) + bias)
    s = jnp.concatenate(chunk_out, axis=0)
    out = x + _mm(u * s * _silu(gate), wout_ref[...])
    o_ref[...] = _rmsnorm(out, fg_ref[...])


def _gmlp_layer(h, ng, w_in, v_g, w_s, bs_full, w_out, final_g, *, tm):
    bsz, t, d = h.shape
    row = _row_spec(tm, d)
    vec = _full_spec((1, d))
    return pl.pallas_call(
        _gmlp_body,
        grid=(bsz, t // tm),
        in_specs=[row, vec, _full_spec(w_in.shape), vec, _full_spec(w_s.shape), _full_spec(bs_full.shape),
                  _full_spec(w_out.shape), vec],
        out_specs=row,
        out_shape=jax.ShapeDtypeStruct((bsz, t, d), F32),
        compiler_params=_params(2),
        name="gmlp_layer",
    )(h, ng, w_in, v_g, w_s, bs_full, w_out, final_g)


def _rwkv_layer(h, ng, w_in, mu, w0, w_w2, a0, w_a2, k_k, k_a, r_k, gn_g, gn_b, w_out):
    d = h.shape[-1]
    nh = d // RWKV_HEAD
    zeros = jnp.zeros((RWKV_LORA, d), F32)
    ww_pad = jnp.concatenate([w_w2, zeros], axis=0).astype(BF16)
    wa_pad = jnp.concatenate([zeros, w_a2], axis=0).astype(BF16)
    e = (jnp.arange(d)[:, None] // RWKV_HEAD == jnp.arange(LANES)[None, :]).astype(BF16)
    et = jnp.concatenate([e.T, e.T], axis=0)
    vec = lambda z: z.reshape(1, -1)
    r, k, v, g, kk, bv, gate = _rwkv_prep(h, vec(ng), w_in.astype(BF16), vec(mu), vec(w0), ww_pad, vec(a0), wa_pad,
                                          vec(k_k), vec(k_a), e, et, tm=256)
    y = _rwkv_scan(r, k, v, g, kk, bv, tm=128, n_pairs=8)
    del nh
    return _rwkv_out(h, y, r, k, v, gate, vec(r_k), vec(gn_g), vec(gn_b), w_out.astype(BF16), e, et, tm=512)


def kernel(x, norm_g, final_g, rwkv_w_in, rwkv_mu, rwkv_w0, rwkv_w_w2, rwkv_a0, rwkv_w_a2, rwkv_k_k, rwkv_k_a, rwkv_r_k, rwkv_gn_g, rwkv_gn_b, rwkv_w_out, hgrn_lb_logits, hgrn_w_in, hgrn_gn_g, hgrn_w_out, conv_w_in, conv_w, conv_w_out, gmlp_w_in, gmlp_v_g, gmlp_w_s, gmlp_b_s, gmlp_w_out):
    depth = norm_g.shape[0]
    assert depth == 4, "the fused final norm assumes the gMLP layer is the last one"
    vec = lambda z: z.reshape(1, -1)
    h = x
    h = _rwkv_layer(h, norm_g[0], rwkv_w_in[0], rwkv_mu[0], rwkv_w0[0], rwkv_w_w2[0], rwkv_a0[0], rwkv_w_a2[0],
                    rwkv_k_k[0], rwkv_k_a[0], rwkv_r_k[0], rwkv_gn_g[0], rwkv_gn_b[0], rwkv_w_out[0])
    h = _hgrn_layer(h, vec(norm_g[1]), hgrn_w_in[0].astype(BF16), hgrn_lb_logits, vec(hgrn_gn_g[0]),
                    hgrn_w_out[0].astype(BF16), layer=1, tm=256)
    h = _conv_layer(h, vec(norm_g[2]), conv_w_in[0].astype(BF16), conv_w[0], conv_w_out[0].astype(BF16), tm=256)
    bs_full = jnp.repeat(gmlp_b_s[0].T, GMLP_GROUP, axis=1)
    h = _gmlp_layer(h, vec(norm_g[3]), gmlp_w_in[0].astype(BF16), vec(gmlp_v_g[0]), gmlp_w_s[0].astype(BF16),
                    bs_full, gmlp_w_out[0].astype(BF16), vec(final_g), tm=256)
    return h
```

```python
import functools

import jax
import jax.numpy as jnp
from jax import lax
from jax.experimental import pallas as pl
from jax.experimental.pallas import tpu as pltpu

F32 = jnp.float32
BF16 = jnp.bfloat16

RMS_EPS = 1e-6
RWKV_HEAD = 64
RWKV_GN_EPS = 64e-5
RWKV_LORA = 64
DECAY_SCALE = 0.6065306597126334
HGRN_HEAD = 128
GMLP_CHUNK = 128
GMLP_GROUP = 128
CONV_WIDTH = 3

LANES = 128
SUBLANES = 8
CHUNK = 64
HALF = CHUNK // 2
SUB_ROWS = 256
SCAN_GROUP_CHUNKS = 2
VMEM_LIMIT = 56 * 1024 * 1024


def _mm(a, b):
    return jnp.dot(a.astype(BF16), b.astype(BF16), preferred_element_type=F32)


def _mm_nt(a, b):
    return lax.dot_general(a.astype(BF16), b.astype(BF16), (((1,), (1,)), ((), ())),
                           preferred_element_type=F32)


def _mm_tn(a, b):
    return lax.dot_general(a.astype(BF16), b.astype(BF16), (((0,), (0,)), ((), ())),
                           preferred_element_type=F32)


def _split2(x):
    hi = x.astype(BF16)
    lo = (x - hi.astype(F32)).astype(BF16)
    return hi, lo


def _split3(x):
    hi = x.astype(BF16)
    r1 = x - hi.astype(F32)
    mid = r1.astype(BF16)
    lo = (r1 - mid.astype(F32)).astype(BF16)
    return hi, mid, lo


def _head_sums(x, e, et2):
    return _head_sums_expand(_head_sums_reduce(x, e), et2)


def _head_sums_reduce(x, e):
    return jnp.dot(x.astype(BF16), e, preferred_element_type=F32)


def _head_sums_expand(s, et2):
    hi, lo = _split2(s)
    return jnp.dot(jnp.concatenate([hi, lo], axis=1), et2, preferred_element_type=F32)


def _cumsum_rows_fused(tri3, g):
    return jnp.dot(tri3, jnp.concatenate(_split3(g), axis=0), preferred_element_type=F32)


def _sigmoid(x):
    return 0.5 * jnp.tanh(0.5 * x) + 0.5


def _silu(x):
    return x * _sigmoid(x)


def _log1pexp_neg(z):
    return jnp.log(1.0 + jnp.exp(-z))


def _softplus(x):
    return jnp.maximum(x, 0.0) + _log1pexp_neg(jnp.abs(x))


def _rmsnorm(x, g):
    ms = jnp.mean(x * x, axis=-1, keepdims=True)
    return x * lax.rsqrt(ms + RMS_EPS) * g


def _iota2(shape, axis):
    return lax.broadcasted_iota(jnp.int32, shape, axis)


def _shift_rows(x, tail, n):
    rolled = pltpu.roll(x, n, 0)
    head = rolled[0:SUBLANES]
    row = _iota2(head.shape, 0)
    head = jnp.where(row < n, pltpu.roll(tail, n, 0), head)
    return jnp.concatenate([head, rolled[SUBLANES:]], axis=0)


def _tri_ones(n):
    return (_iota2((n, n), 0) >= _iota2((n, n), 1)).astype(BF16)


def _row_spec(tm, d):
    return pl.BlockSpec((None, tm, d), lambda b, t: (b, t, 0))


def _full_spec(shape):
    n = len(shape)
    return pl.BlockSpec(shape, lambda *_: (0,) * n)


def _params(n_axes):
    return pltpu.CompilerParams(dimension_semantics=("arbitrary",) * n_axes, vmem_limit_bytes=VMEM_LIMIT)


def _rwkv_prep_body(h_ref, ng_ref, win_ref, mu_ref, w0_ref, ww_ref, a0_ref, wa_ref, kk_ref, ka_ref,
                    e_ref, et_ref,
                    r_out, k_out, v_out, g_out, kk_out, bv_out, gate_out, carry_ref, *, d):
    tm = h_ref.shape[0]

    @pl.when(pl.program_id(1) == 0)
    def _():
        carry_ref[...] = jnp.zeros_like(carry_ref)

    for s0 in range(0, tm, SUB_ROWS):
        rows = pl.ds(s0, SUB_ROWS)
        hn = _rmsnorm(h_ref[rows, :], ng_ref[...]).astype(BF16)

        def project(lo, hi):
            p = jnp.dot(hn, win_ref[:, lo:hi], preferred_element_type=F32)
            prev = _shift_rows(p, carry_ref[:, lo:hi], 1)
            carry_ref[:, lo:hi] = p[SUB_ROWS - SUBLANES:SUB_ROWS, :]
            return p + (prev - p) * mu_ref[:, lo:hi]

        lw = project(3 * d, 3 * d + 2 * RWKV_LORA)
        k = project(d, 2 * d)
        lw = jnp.where(_iota2(lw.shape, 1) < RWKV_LORA, jnp.tanh(lw), lw)
        dw = _mm(lw, ww_ref[...])
        da = _mm(lw, wa_ref[...])
        r_out[rows, :] = project(0, d)
        kk = k * kk_ref[...]
        n2 = _head_sums_reduce(kk * kk, e_ref[...])
        v_out[rows, :] = project(2 * d, 3 * d)
        n2 = _head_sums_expand(n2, et_ref[...])
        gate_out[rows, :] = project(3 * d + 2 * RWKV_LORA, win_ref.shape[1])
        a = _sigmoid(a0_ref[...] + da)
        kk = kk * lax.rsqrt(jnp.maximum(n2, 1e-24))
        k_out[rows, :] = k * (1.0 + (a - 1.0) * ka_ref[...])
        g_out[rows, :] = -DECAY_SCALE * _sigmoid(w0_ref[...] + dw)
        kk_out[rows, :] = kk
        bv_out[rows, :] = kk * a


def _rwkv_prep(h, ng, w_in, mu, w0, ww_pad, a0, wa_pad, k_k, k_a, e, et, *, tm):
    bsz, t, d = h.shape
    cols = w_in.shape[1]
    row = _row_spec(tm, d)
    vec = _full_spec((1, d))
    out = jax.ShapeDtypeStruct((bsz, t, d), F32)
    return pl.pallas_call(
        functools.partial(_rwkv_prep_body, d=d),
        grid=(bsz, t // tm),
        in_specs=[row, vec, _full_spec(w_in.shape), _full_spec((1, cols)), vec, _full_spec(ww_pad.shape), vec,
                  _full_spec(wa_pad.shape), vec, vec, _full_spec(e.shape), _full_spec(et.shape)],
        out_specs=[row] * 7,
        out_shape=[out] * 7,
        scratch_shapes=[pltpu.VMEM((SUBLANES, cols), F32)],
        compiler_params=_params(2),
        name="rwkv_prep",
    )(h, ng, w_in, mu, w0, ww_pad, a0, wa_pad, k_k, k_a, e, et)


def _stack_heads(x, m0):
    return jnp.concatenate([jnp.where(m0, x, 0.0), jnp.where(m0, 0.0, x)], axis=0)


def _unit_lower_inverse(lows, i2, j2):
    s = 1
    eye = jnp.where(i2 == j2, 1.0, 0.0)
    t_inv = [eye for _ in lows]
    while s < CHUNK:
        keep = ((i2 // (2 * s)) == (j2 // (2 * s))) & ((i2 // s) != (j2 // s))
        subs = [jnp.where(keep, low, 0.0) for low in lows]
        if s == 1:
            t_inv = [t + sub for t, sub in zip(t_inv, subs)]
        else:
            left = [_mm(t, sub) for t, sub in zip(t_inv, subs)]
            t_inv = [t + _mm(lt, t) for t, lt in zip(t_inv, left)]
        s *= 2
    return t_inv


def _rwkv_scan_body(r_ref, k_ref, v_ref, g_ref, kk_ref, bv_ref, y_ref, state_ref):
    tm = r_ref.shape[0]
    c = CHUNK
    w = 2 * RWKV_HEAD
    nc = tm // c
    n_pairs = r_ref.shape[1] // w

    @pl.when(pl.program_id(2) == 0)
    def _():
        state_ref[...] = jnp.zeros_like(state_ref)

    m0 = _iota2((c, w), 1) < RWKV_HEAD
    m0b = m0
    tri3 = jnp.concatenate([_tri_ones(c)] * 3, axis=1)
    i2 = _iota2((2 * c, w), 0)
    j2 = _iota2((2 * c, w), 1)
    same_head = (i2 // RWKV_HEAD) == (j2 // RWKV_HEAD)
    ir = _iota2((c, w), 0)
    jr = _iota2((c, w), 1) % c
    ir2 = _iota2((c, 2 * w), 0)
    jr2 = _iota2((c, 2 * w), 1) % c
    stack = functools.partial(_stack_heads, m0=m0)
    each = lambda f, *lists: [f(*xs) for xs in zip(*lists)]
    h_state = [state_ref[pi] for pi in range(n_pairs)]
    for c0 in range(0, nc, SCAN_GROUP_CHUNKS):
        where = [(pl.ds(ci * c, c), pl.ds(pi * w, w)) for ci in range(c0, min(c0 + SCAN_GROUP_CHUNKS, nc))
                 for pi in range(n_pairs)]
        load = lambda ref: [ref[rows, cols] for rows, cols in where]

        r, k, v, g, kk, bv = load(r_ref), load(k_ref), load(v_ref), load(g_ref), load(kk_ref), load(bv_ref)
        bc = each(lambda x: _cumsum_rows_fused(tri3, x), g)
        bl = each(lambda x: x[c - 1:c, :], bc)
        p_inv = each(lambda x: jnp.exp(-x), bc)
        p_end = each(lambda x, y: jnp.exp(x - y), bl, bc)
        rt = each(lambda a, x: a * jnp.exp(x), r, bc)
        at2 = each(lambda a, x, y: stack(-a * jnp.exp(x - y)).astype(BF16), kk, bc, g)
        v2 = each(lambda a: stack(a).astype(BF16), v)
        lhs = each(lambda a, x: jnp.concatenate([a, x.astype(BF16)], axis=0), at2, rt)
        rhs = each(lambda a, x, p: jnp.concatenate([stack(a * p), stack(x * p)], axis=0).astype(BF16), bv, k, p_inv)
        end = each(lambda a, x, p: jnp.concatenate([a * p, x * p], axis=0).astype(BF16), bv, k, p_end)

        prod = each(_mm_nt, lhs, rhs)
        low = each(lambda x: jnp.where(j2 < i2, x[0:2 * c, 0:w], 0.0), prod)
        a_ak = each(lambda x: jnp.where(jr < ir, x[0:c, w:2 * w] + x[c:2 * c, w:2 * w], 0.0).astype(BF16), prod)
        a_r = each(lambda x: jnp.where(jr2 <= ir2, x[2 * c:3 * c, :], 0.0).astype(BF16), prod)
        t_inv = _unit_lower_inverse(low, i2, j2)
        t_row = each(lambda t: (t[0:c, :] + t[c:2 * c, :]).astype(BF16), t_inv)
        z = each(_mm, a_ak, v2)
        w12 = each(lambda t, a, x: _mm(t, jnp.concatenate([a, stack(x).astype(BF16)], axis=1)), t_row, at2, z)
        w12b = each(lambda x: x.astype(BF16), w12)
        zero_v = jnp.zeros((2 * c, w), BF16)
        q12 = each(lambda a, x, y: _mm(a, jnp.concatenate(
            [jnp.concatenate([_stack_heads(x[:, 0:w], m0b), _stack_heads(x[:, w:2 * w], m0b)], axis=1),
             jnp.concatenate([zero_v, y], axis=1)], axis=0)), a_r, w12b, v2)
        q1 = each(lambda a, x: a + x[:, 0:w], rt, q12)
        q2 = each(lambda x: x[:, w:2 * w], q12)
        zero_c = jnp.zeros((c, w), BF16)
        ti = each(lambda e, x, y: _mm_tn(e, jnp.concatenate(
            [x, jnp.concatenate([zero_c, y.astype(BF16)], axis=1)], axis=0)), end, w12b, v)
        trans = each(lambda x, e: jnp.where(same_head, x[:, 0:w], 0.0)
                     + jnp.where(i2 == j2, jnp.broadcast_to(jnp.exp(e), (2 * c, w)), 0.0), ti, bl)
        inject = each(lambda x: jnp.where(same_head, x[:, w:2 * w], 0.0), ti)

        for n, (rows, cols) in enumerate(where):
            pi = n % n_pairs
            y_ref[rows, cols] = _mm(q1[n], h_state[pi]) + q2[n]
            h_state[pi] = _mm(trans[n], h_state[pi]) + inject[n]
    for pi in range(n_pairs):
        state_ref[pi] = h_state[pi]


def _rwkv_scan(r, k, v, g, kk, bv, *, tm, n_pairs):
    bsz, t, d = r.shape
    w = 2 * RWKV_HEAD
    spec = pl.BlockSpec((None, tm, w * n_pairs), lambda b, p, i: (b, i, p))
    return pl.pallas_call(
        _rwkv_scan_body,
        grid=(bsz, d // (w * n_pairs), t // tm),
        in_specs=[spec] * 6,
        out_specs=spec,
        out_shape=jax.ShapeDtypeStruct((bsz, t, d), F32),
        scratch_shapes=[pltpu.VMEM((n_pairs, w, w), F32)],
        compiler_params=_params(3),
        name="rwkv_scan",
    )(r, k, v, g, kk, bv)


def _rwkv_out_body(h_ref, y_ref, r_ref, k_ref, v_ref, gate_ref, rk_ref, gg_ref, gb_ref, wout_ref,
                   e_ref, et_ref, o_ref):
    tm = h_ref.shape[0]
    subs = [pl.ds(s0, SUB_ROWS) for s0 in range(0, tm, SUB_ROWS)]
    each = lambda f, *lists: [f(*xs) for xs in zip(*lists)]
    reduce_ = lambda z: _head_sums_reduce(z, e_ref[...])
    expand = lambda s: _head_sums_expand(s, et_ref[...])
    y = [y_ref[rows, :] for rows in subs]
    rk = [r_ref[rows, :] * k_ref[rows, :] * rk_ref[...] for rows in subs]
    y_sum, rk_sum = each(reduce_, y), each(reduce_, rk)
    y_sum, rk_sum = each(expand, y_sum), each(expand, rk_sum)
    dy = each(lambda a, s: a - s * (1.0 / RWKV_HEAD), y, y_sum)
    var = each(expand, each(lambda a: reduce_(a * a), dy))
    for rows, dy_, var_, rk_ in zip(subs, dy, var, rk_sum):
        yn = dy_ * lax.rsqrt(var_ * (1.0 / RWKV_HEAD) + RWKV_GN_EPS) * gg_ref[...] + gb_ref[...]
        out = (yn + rk_ * v_ref[rows, :]) * _silu(gate_ref[rows, :])
        o_ref[rows, :] = h_ref[rows, :] + _mm(out, wout_ref[...])


def _rwkv_out(h, y, r, k, v, gate, r_k, gn_g, gn_b, w_out, e, et, *, tm):
    bsz, t, d = h.shape
    row = _row_spec(tm, d)
    vec = _full_spec((1, d))
    return pl.pallas_call(
        _rwkv_out_body,
        grid=(bsz, t // tm),
        in_specs=[row] * 6 + [vec, vec, vec, _full_spec(w_out.shape), _full_spec(e.shape), _full_spec(et.shape)],
        out_specs=row,
        out_shape=jax.ShapeDtypeStruct((bsz, t, d), F32),
        compiler_params=_params(2),
        name="rwkv_out",
    )(h, y, r, k, v, gate, r_k, gn_g, gn_b, w_out, e, et)


def _hgrn_body(h_ref, ng_ref, win_ref, lbl_ref, gg_ref, wout_ref, o_ref, state_ref, *, layer):
    tm, d = h_ref.shape
    c = CHUNK
    nh = d // HGRN_HEAD

    @pl.when(pl.program_id(1) == 0)
    def _():
        state_ref[...] = jnp.zeros_like(state_ref)

    logits = lbl_ref[...]
    ex = jnp.exp(logits - jnp.max(logits, axis=0, keepdims=True))
    lb = jnp.sum(ex[1:layer + 1, :], axis=0, keepdims=True) / jnp.sum(ex, axis=0, keepdims=True)

    nc = SUB_ROWS // c
    tri3 = jnp.concatenate([_tri_ones(c)] * 3, axis=1)
    row = _iota2((c, HGRN_HEAD), 0)
    top = row < HALF
    ri, ci_ = _iota2((c, c), 0), _iota2((c, c), 1)
    diag_blocks = (ri // HALF) == (ci_ // HALF)
    causal = ri >= ci_
    each = lambda f, *lists: [f(*xs) for xs in zip(*lists)]
    where = [(slice(ci * c, (ci + 1) * c), slice(hi * HGRN_HEAD, (hi + 1) * HGRN_HEAD))
             for ci in range(nc) for hi in range(nh)]
    take = lambda z: [z[rows, cols] for rows, cols in where]
    s_t = [state_ref[hi] for hi in range(nh)]

    subs = [pl.ds(s0, SUB_ROWS) for s0 in range(0, tm, SUB_ROWS)]
    xs = [h_ref[rows, :] for rows in subs]
    ps = [_mm(_rmsnorm(x, ng_ref[...]), win_ref[...]) for x in xs]
    for sub_rows, x, p in zip(subs, xs, ps):
        q = p[:, 0:d]
        f_pre = p[:, d:2 * d]
        v = p[:, 2 * d:3 * d]
        gate = p[:, 3 * d:4 * d]
        sig = _sigmoid(f_pre)
        log_f = jnp.log(lb + (1.0 - lb) * sig)
        k = (1.0 - lb) * (1.0 - sig)
        qs, ks, vs, gs = take(q), take(k), take(v), take(log_f)
        b = each(lambda x: _cumsum_rows_fused(tri3, x), gs)
        bl = each(lambda x: x[c - 1:c], b)
        beta = each(lambda x: x[HALF - 1:HALF], b)
        mid = each(lambda x: jnp.where(top, x[HALF // 2:HALF // 2 + 1], x[HALF + HALF // 2:HALF + HALF // 2 + 1]), b)
        q_d = each(lambda a, x, m: (a * jnp.exp(x - m)).astype(BF16), qs, b, mid)
        k_d = each(lambda a, x, m: (a * jnp.exp(m - x)).astype(BF16), ks, b, mid)
        q_o = each(lambda a, x, e: jnp.where(top, 0.0, a * jnp.exp(jnp.minimum(x - e, 0.0))).astype(BF16), qs, b, beta)
        k_o = each(lambda a, x, e: jnp.where(top, a * jnp.exp(jnp.minimum(e - x, 0.0)), 0.0).astype(BF16), ks, b, beta)
        q_in = each(lambda a, x: (a * jnp.exp(x)).astype(BF16), qs, b)
        k_out = each(lambda a, x, e: (a * jnp.exp(e - x)).astype(BF16), ks, b, bl)
        v_b = each(lambda a: a.astype(BF16), vs)
        att = each(lambda qd, kd, qo, ko: jnp.where(diag_blocks, jnp.where(causal, _mm_nt(qd, kd), 0.0), _mm_nt(qo, ko)),
                   q_d, k_d, q_o, k_o)
        o_intra = each(_mm, att, v_b)
        kv = each(_mm_tn, v_b, k_out)
        s_in = []
        for n in range(len(where)):
            hi = n % nh
            s_in.append(s_t[hi])
            s_t[hi] = s_t[hi] * jnp.exp(bl[n]) + kv[n]
        o = each(lambda oi, a, s: oi + _mm_nt(a, s), o_intra, q_in, s_in)
        o = each(lambda z: z * lax.rsqrt(jnp.mean(z * z, axis=-1, keepdims=True) + RMS_EPS), o)
        o_all = jnp.concatenate([jnp.concatenate(o[ci * nh:(ci + 1) * nh], axis=1) for ci in range(nc)], axis=0)
        o_all = o_all * gg_ref[...] * _silu(gate)
        o_ref[sub_rows, :] = x + _mm(o_all, wout_ref[...])
    for hi in range(nh):
        state_ref[hi] = s_t[hi]


def _hgrn_layer(h, ng, w_in, lb_logits, gn_g, w_out, *, layer, tm):
    bsz, t, d = h.shape
    row = _row_spec(tm, d)
    vec = _full_spec((1, d))
    return pl.pallas_call(
        functools.partial(_hgrn_body, layer=layer),
        grid=(bsz, t // tm),
        in_specs=[row, vec, _full_spec(w_in.shape), _full_spec(lb_logits.shape), vec, _full_spec(w_out.shape)],
        out_specs=row,
        out_shape=jax.ShapeDtypeStruct((bsz, t, d), F32),
        scratch_shapes=[pltpu.VMEM((d // HGRN_HEAD, HGRN_HEAD, HGRN_HEAD), F32)],
        compiler_params=_params(2),
        name="hgrn_layer",
    )(h, ng, w_in, lb_logits, gn_g, w_out)


def _conv_body(h_ref, ng_ref, win_ref, cw_ref, wout_ref, o_ref, carry_ref):
    tm, d = h_ref.shape

    @pl.when(pl.program_id(1) == 0)
    def _():
        carry_ref[...] = jnp.zeros_like(carry_ref)

    subs = [pl.ds(s0, SUB_ROWS) for s0 in range(0, tm, SUB_ROWS)]
    xs = [h_ref[rows, :] for rows in subs]
    ps = [_mm(_rmsnorm(x, ng_ref[...]), win_ref[...]) for x in xs]
    tail = carry_ref[...]
    for rows, x, p in zip(subs, xs, ps):
        b_gate = p[:, 0:d]
        y = p[:, d:2 * d] * p[:, 2 * d:3 * d]
        gate = p[:, 3 * d:4 * d]
        y1 = _shift_rows(y, tail, 1)
        y2 = _shift_rows(y, tail, 2)
        tail = y[SUB_ROWS - SUBLANES:SUB_ROWS, :]
        yc = cw_ref[0:1, :] * y2 + cw_ref[1:2, :] * y1 + cw_ref[2:3, :] * y
        o_ref[rows, :] = x + _mm(b_gate * yc * _silu(gate), wout_ref[...])
    carry_ref[...] = tail


def _conv_layer(h, ng, w_in, conv_w, w_out, *, tm):
    bsz, t, d = h.shape
    row = _row_spec(tm, d)
    return pl.pallas_call(
        _conv_body,
        grid=(bsz, t // tm),
        in_specs=[row, _full_spec((1, d)), _full_spec(w_in.shape), _full_spec(conv_w.shape), _full_spec(w_out.shape)],
        out_specs=row,
        out_shape=jax.ShapeDtypeStruct((bsz, t, d), F32),
        scratch_shapes=[pltpu.VMEM((SUBLANES, d), F32)],
        compiler_params=_params(2),
        name="conv_layer",
    )(h, ng, w_in, conv_w, w_out)


def _gmlp_body(h_ref, ng_ref, win_ref, vg_ref, ws_ref, bs_ref, wout_ref, fg_ref, o_ref):
    tm, d = h_ref.shape
    c = GMLP_CHUNK
    causal = _iota2((c, c), 0) >= _iota2((c, c), 1)
    bias = bs_ref[...]
    nc = SUB_ROWS // c
    subs = [pl.ds(s0, SUB_ROWS) for s0 in range(0, tm, SUB_ROWS)]
    xs = [h_ref[rows, :] for rows in subs]
    ps = [_mm(_rmsnorm(x, ng_ref[...]), win_ref[...]) for x in xs]
    for rows, x, p in zip(subs, xs, ps):
        u = p[:, 0:d]
        vb = _rmsnorm(p[:, d:2 * d], vg_ref[...]).astype(BF16)
        gate = p[:, 2 * d:3 * d]
        parts = []
        for gi in range(d // GMLP_GROUP):
            cols = slice(gi * GMLP_GROUP, (gi + 1) * GMLP_GROUP)
            v_g = jnp.concatenate([vb[ci * c:(ci + 1) * c, cols] for ci in range(nc)], axis=1)
            parts.append(_mm(jnp.where(causal, ws_ref[gi], 0.0), v_g))
        s = jnp.concatenate(
            [jnp.concatenate([q[:, ci * GMLP_GROUP:(ci + 1) * GMLP_GROUP] for q in parts], axis=1) + bias
             for ci in range(nc)], axis=0)
        out = x + _mm(u * s * _silu(gate), wout_ref[...])
        o_ref[rows, :] = _rmsnorm(out, fg_ref[...])


def _gmlp_layer(h, ng, w_in, v_g, w_s, bs_full, w_out, final_g, *, tm):
    bsz, t, d = h.shape
    row = _row_spec(tm, d)
    vec = _full_spec((1, d))
    return pl.pallas_call(
        _gmlp_body,
        grid=(bsz, t // tm),
        in_specs=[row, vec, _full_spec(w_in.shape), vec, _full_spec(w_s.shape), _full_spec(bs_full.shape),
                  _full_spec(w_out.shape), vec],
        out_specs=row,
        out_shape=jax.ShapeDtypeStruct((bsz, t, d), F32),
        compiler_params=_params(2),
        name="gmlp_layer",
    )(h, ng, w_in, v_g, w_s, bs_full, w_out, final_g)


def _rwkv_layer(h, ng, w_in, mu, w0, w_w2, a0, w_a2, k_k, k_a, r_k, gn_g, gn_b, w_out):
    d = h.shape[-1]
    nh = d // RWKV_HEAD
    zeros = jnp.zeros((RWKV_LORA, d), F32)
    ww_pad = jnp.concatenate([w_w2, zeros], axis=0).astype(BF16)
    wa_pad = jnp.concatenate([zeros, w_a2], axis=0).astype(BF16)
    e = (jnp.arange(d)[:, None] // RWKV_HEAD == jnp.arange(LANES)[None, :]).astype(BF16)
    et = jnp.concatenate([e.T, e.T], axis=0)
    vec = lambda z: z.reshape(1, -1)
    r, k, v, g, kk, bv, gate = _rwkv_prep(h, vec(ng), w_in.astype(BF16), vec(mu), vec(w0), ww_pad, vec(a0), wa_pad,
                                          vec(k_k), vec(k_a), e, et, tm=512)
    y = _rwkv_scan(r, k, v, g, kk, bv, tm=256, n_pairs=8)
    del nh
    return _rwkv_out(h, y, r, k, v, gate, vec(r_k), vec(gn_g), vec(gn_b), w_out.astype(BF16), e, et, tm=512)


def kernel(x, norm_g, final_g, rwkv_w_in, rwkv_mu, rwkv_w0, rwkv_w_w2, rwkv_a0, rwkv_w_a2, rwkv_k_k, rwkv_k_a, rwkv_r_k, rwkv_gn_g, rwkv_gn_b, rwkv_w_out, hgrn_lb_logits, hgrn_w_in, hgrn_gn_g, hgrn_w_out, conv_w_in, conv_w, conv_w_out, gmlp_w_in, gmlp_v_g, gmlp_w_s, gmlp_b_s, gmlp_w_out):
    depth = norm_g.shape[0]
    assert depth == 4, "the fused final norm assumes the gMLP layer is the last one"
    vec = lambda z: z.reshape(1, -1)
    h = x
    h = _rwkv_layer(h, norm_g[0], rwkv_w_in[0], rwkv_mu[0], rwkv_w0[0], rwkv_w_w2[0], rwkv_a0[0], rwkv_w_a2[0],
                    rwkv_k_k[0], rwkv_k_a[0], rwkv_r_k[0], rwkv_gn_g[0], rwkv_gn_b[0], rwkv_w_out[0])
    h = _hgrn_layer(h, vec(norm_g[1]), hgrn_w_in[0].astype(BF16), hgrn_lb_logits, vec(hgrn_gn_g[0]),
                    hgrn_w_out[0].astype(BF16), layer=1, tm=512)
    h = _conv_layer(h, vec(norm_g[2]), conv_w_in[0].astype(BF16), conv_w[0], conv_w_out[0].astype(BF16), tm=512)
    bs_full = jnp.repeat(gmlp_b_s[0].T, GMLP_GROUP, axis=1)
    h = _gmlp_layer(h, vec(norm_g[3]), gmlp_w_in[0].astype(BF16), vec(gmlp_v_g[0]), gmlp_w_s[0].astype(BF16),
                    bs_full, gmlp_w_out[0].astype(BF16), vec(final_g), tm=512)
    return h
```

```python
import functools

import jax
import jax.numpy as jnp
from jax import lax
from jax.experimental import pallas as pl
from jax.experimental.pallas import tpu as pltpu

F32 = jnp.float32
BF16 = jnp.bfloat16

RMS_EPS = 1e-6
RWKV_HEAD = 64
RWKV_GN_EPS = 64e-5
RWKV_LORA = 64
DECAY_SCALE = 0.6065306597126334
HGRN_HEAD = 128
GMLP_CHUNK = 128
GMLP_GROUP = 128
CONV_WIDTH = 3

LANES = 128
SUBLANES = 8
CHUNK = 64
HALF = CHUNK // 2
SUB_ROWS = 256
SCAN_GROUP_CHUNKS = 2
VMEM_LIMIT = 56 * 1024 * 1024


def _mm(a, b):
    return jnp.dot(a.astype(BF16), b.astype(BF16), preferred_element_type=F32)


def _mm_nt(a, b):
    return lax.dot_general(a.astype(BF16), b.astype(BF16), (((1,), (1,)), ((), ())),
                           preferred_element_type=F32)


def _mm_tn(a, b):
    return lax.dot_general(a.astype(BF16), b.astype(BF16), (((0,), (0,)), ((), ())),
                           preferred_element_type=F32)


def _split2(x):
    hi = x.astype(BF16)
    lo = (x - hi.astype(F32)).astype(BF16)
    return hi, lo


def _split3(x):
    hi = x.astype(BF16)
    r1 = x - hi.astype(F32)
    mid = r1.astype(BF16)
    lo = (r1 - mid.astype(F32)).astype(BF16)
    return hi, mid, lo


def _head_sums(x, e, et2):
    return _head_sums_expand(_head_sums_reduce(x, e), et2)


def _head_sums_reduce(x, e):
    return jnp.dot(x.astype(BF16), e, preferred_element_type=F32)


def _head_sums_expand(s, et2):
    hi, lo = _split2(s)
    return jnp.dot(jnp.concatenate([hi, lo], axis=1), et2, preferred_element_type=F32)


def _cumsum_rows_fused(tri3, g):
    return jnp.dot(tri3, jnp.concatenate(_split3(g), axis=0), preferred_element_type=F32)


def _sigmoid(x):
    return 0.5 * jnp.tanh(0.5 * x) + 0.5


def _silu(x):
    return x * _sigmoid(x)


def _log1pexp_neg(z):
    return jnp.log(1.0 + jnp.exp(-z))


def _softplus(x):
    return jnp.maximum(x, 0.0) + _log1pexp_neg(jnp.abs(x))


def _rmsnorm(x, g):
    ms = jnp.mean(x * x, axis=-1, keepdims=True)
    return x * lax.rsqrt(ms + RMS_EPS) * g


def _iota2(shape, axis):
    return lax.broadcasted_iota(jnp.int32, shape, axis)


def _shift_rows(x, tail, n):
    rolled = pltpu.roll(x, n, 0)
    head = rolled[0:SUBLANES]
    row = _iota2(head.shape, 0)
    head = jnp.where(row < n, pltpu.roll(tail, n, 0), head)
    return jnp.concatenate([head, rolled[SUBLANES:]], axis=0)


def _tri_ones(n):
    return (_iota2((n, n), 0) >= _iota2((n, n), 1)).astype(BF16)


def _row_spec(tm, d):
    return pl.BlockSpec((None, tm, d), lambda b, t: (b, t, 0))


def _full_spec(shape):
    n = len(shape)
    return pl.BlockSpec(shape, lambda *_: (0,) * n)


def _params(n_axes):
    return pltpu.CompilerParams(dimension_semantics=("arbitrary",) * n_axes, vmem_limit_bytes=VMEM_LIMIT)


def _rwkv_prep_body(h_ref, ng_ref, win_ref, mu_ref, w0_ref, ww_ref, a0_ref, wa_ref, kk_ref, ka_ref,
                    e_ref, et_ref,
                    r_out, k_out, v_out, g_out, kk_out, bv_out, gate_out, carry_ref, *, d):
    tm = h_ref.shape[0]

    @pl.when(pl.program_id(1) == 0)
    def _():
        carry_ref[...] = jnp.zeros_like(carry_ref)

    for s0 in range(0, tm, SUB_ROWS):
        rows = pl.ds(s0, SUB_ROWS)
        hn = _rmsnorm(h_ref[rows, :], ng_ref[...]).astype(BF16)

        def project(lo, hi):
            p = jnp.dot(hn, win_ref[:, lo:hi], preferred_element_type=F32)
            prev = _shift_rows(p, carry_ref[:, lo:hi], 1)
            carry_ref[:, lo:hi] = p[SUB_ROWS - SUBLANES:SUB_ROWS, :]
            return p + (prev - p) * mu_ref[:, lo:hi]

        lw = project(3 * d, 3 * d + 2 * RWKV_LORA)
        k = project(d, 2 * d)
        lw = jnp.where(_iota2(lw.shape, 1) < RWKV_LORA, jnp.tanh(lw), lw)
        dw = _mm(lw, ww_ref[...])
        da = _mm(lw, wa_ref[...])
        r_out[rows, :] = project(0, d)
        kk = k * kk_ref[...]
        n2 = _head_sums_reduce(kk * kk, e_ref[...])
        v_out[rows, :] = project(2 * d, 3 * d)
        n2 = _head_sums_expand(n2, et_ref[...])
        gate_out[rows, :] = project(3 * d + 2 * RWKV_LORA, win_ref.shape[1])
        a = _sigmoid(a0_ref[...] + da)
        kk = kk * lax.rsqrt(jnp.maximum(n2, 1e-24))
        k_out[rows, :] = k * (1.0 + (a - 1.0) * ka_ref[...])
        g_out[rows, :] = -DECAY_SCALE * _sigmoid(w0_ref[...] + dw)
        kk_out[rows, :] = kk
        bv_out[rows, :] = kk * a


def _rwkv_prep(h, ng, w_in, mu, w0, ww_pad, a0, wa_pad, k_k, k_a, e, et, *, tm):
    bsz, t, d = h.shape
    cols = w_in.shape[1]
    row = _row_spec(tm, d)
    vec = _full_spec((1, d))
    out = jax.ShapeDtypeStruct((bsz, t, d), F32)
    return pl.pallas_call(
        functools.partial(_rwkv_prep_body, d=d),
        grid=(bsz, t // tm),
        in_specs=[row, vec, _full_spec(w_in.shape), _full_spec((1, cols)), vec, _full_spec(ww_pad.shape), vec,
                  _full_spec(wa_pad.shape), vec, vec, _full_spec(e.shape), _full_spec(et.shape)],
        out_specs=[row] * 7,
        out_shape=[out] * 7,
        scratch_shapes=[pltpu.VMEM((SUBLANES, cols), F32)],
        compiler_params=_params(2),
        name="rwkv_prep",
    )(h, ng, w_in, mu, w0, ww_pad, a0, wa_pad, k_k, k_a, e, et)


def _stack_heads(x, m0):
    return jnp.concatenate([jnp.where(m0, x, 0.0), jnp.where(m0, 0.0, x)], axis=0)


def _unit_lower_inverse(lows, i2, j2):
    s = 1
    eye = jnp.where(i2 == j2, 1.0, 0.0)
    t_inv = [eye for _ in lows]
    while s < CHUNK:
        keep = ((i2 // (2 * s)) == (j2 // (2 * s))) & ((i2 // s) != (j2 // s))
        subs = [jnp.where(keep, low, 0.0) for low in lows]
        if s == 1:
            t_inv = [t + sub for t, sub in zip(t_inv, subs)]
        else:
            left = [_mm(t, sub) for t, sub in zip(t_inv, subs)]
            t_inv = [t + _mm(lt, t) for t, lt in zip(t_inv, left)]
        s *= 2
    return t_inv


def _rwkv_scan_body(r_ref, k_ref, v_ref, g_ref, kk_ref, bv_ref, gate_ref, h_ref, rk_ref, gg_ref, gb_ref,
                    wout_ref, e_ref, et_ref, o_ref, state_ref, y_ref):
    tm = r_ref.shape[0]
    c = CHUNK
    w = 2 * RWKV_HEAD
    nc = tm // c
    n_pairs = r_ref.shape[1] // w

    @pl.when(pl.program_id(2) == 0)
    def _():
        state_ref[...] = jnp.zeros_like(state_ref)

    m0 = _iota2((c, w), 1) < RWKV_HEAD
    m0b = m0
    tri3 = jnp.concatenate([_tri_ones(c)] * 3, axis=1)
    i2 = _iota2((2 * c, w), 0)
    j2 = _iota2((2 * c, w), 1)
    same_head = (i2 // RWKV_HEAD) == (j2 // RWKV_HEAD)
    ir = _iota2((c, w), 0)
    jr = _iota2((c, w), 1) % c
    ir2 = _iota2((c, 2 * w), 0)
    jr2 = _iota2((c, 2 * w), 1) % c
    stack = functools.partial(_stack_heads, m0=m0)
    each = lambda f, *lists: [f(*xs) for xs in zip(*lists)]
    h_state = [state_ref[pi] for pi in range(n_pairs)]
    for c0 in range(0, nc, SCAN_GROUP_CHUNKS):
        where = [(pl.ds(ci * c, c), pl.ds(pi * w, w)) for ci in range(c0, min(c0 + SCAN_GROUP_CHUNKS, nc))
                 for pi in range(n_pairs)]
        load = lambda ref: [ref[rows, cols] for rows, cols in where]

        r, k, v, g, kk, bv = load(r_ref), load(k_ref), load(v_ref), load(g_ref), load(kk_ref), load(bv_ref)
        bc = each(lambda x: _cumsum_rows_fused(tri3, x), g)
        bl = each(lambda x: x[c - 1:c, :], bc)
        p_inv = each(lambda x: jnp.exp(-x), bc)
        p_end = each(lambda x, y: jnp.exp(x - y), bl, bc)
        rt = each(lambda a, x: a * jnp.exp(x), r, bc)
        at2 = each(lambda a, x, y: stack(-a * jnp.exp(x - y)).astype(BF16), kk, bc, g)
        v2 = each(lambda a: stack(a).astype(BF16), v)
        lhs = each(lambda a, x: jnp.concatenate([a, x.astype(BF16)], axis=0), at2, rt)
        rhs = each(lambda a, x, p: jnp.concatenate([stack(a * p), stack(x * p)], axis=0).astype(BF16), bv, k, p_inv)
        end = each(lambda a, x, p: jnp.concatenate([a * p, x * p], axis=0).astype(BF16), bv, k, p_end)

        prod = each(_mm_nt, lhs, rhs)
        low = each(lambda x: jnp.where(j2 < i2, x[0:2 * c, 0:w], 0.0), prod)
        a_ak = each(lambda x: jnp.where(jr < ir, x[0:c, w:2 * w] + x[c:2 * c, w:2 * w], 0.0).astype(BF16), prod)
        a_r = each(lambda x: jnp.where(jr2 <= ir2, x[2 * c:3 * c, :], 0.0).astype(BF16), prod)
        t_inv = _unit_lower_inverse(low, i2, j2)
        t_row = each(lambda t: (t[0:c, :] + t[c:2 * c, :]).astype(BF16), t_inv)
        z = each(_mm, a_ak, v2)
        w12 = each(lambda t, a, x: _mm(t, jnp.concatenate([a, stack(x).astype(BF16)], axis=1)), t_row, at2, z)
        w12b = each(lambda x: x.astype(BF16), w12)
        zero_v = jnp.zeros((2 * c, w), BF16)
        q12 = each(lambda a, x, y: _mm(a, jnp.concatenate(
            [jnp.concatenate([_stack_heads(x[:, 0:w], m0b), _stack_heads(x[:, w:2 * w], m0b)], axis=1),
             jnp.concatenate([zero_v, y], axis=1)], axis=0)), a_r, w12b, v2)
        q1 = each(lambda a, x: a + x[:, 0:w], rt, q12)
        q2 = each(lambda x: x[:, w:2 * w], q12)
        zero_c = jnp.zeros((c, w), BF16)
        ti = each(lambda e, x, y: _mm_tn(e, jnp.concatenate(
            [x, jnp.concatenate([zero_c, y.astype(BF16)], axis=1)], axis=0)), end, w12b, v)
        trans = each(lambda x, e: jnp.where(same_head, x[:, 0:w], 0.0)
                     + jnp.where(i2 == j2, jnp.broadcast_to(jnp.exp(e), (2 * c, w)), 0.0), ti, bl)
        inject = each(lambda x: jnp.where(same_head, x[:, w:2 * w], 0.0), ti)
        q1t = each(lambda a, x: jnp.concatenate([a, x], axis=0).astype(BF16), q1, trans)

        for n, (rows, cols) in enumerate(where):
            pi = n % n_pairs
            yh = _mm(q1t[n], h_state[pi])
            y_ref[rows, cols] = yh[0:c, :] + q2[n]
            h_state[pi] = yh[c:3 * c, :] + inject[n]
    for pi in range(n_pairs):
        state_ref[pi] = h_state[pi]
    _rwkv_out_tile(h_ref, y_ref, r_ref, k_ref, v_ref, gate_ref, rk_ref, gg_ref, gb_ref, wout_ref, e_ref, et_ref, o_ref)


def _rwkv_scan(h, r, k, v, g, kk, bv, gate, r_k, gn_g, gn_b, w_out, e, et, *, tm):
    bsz, t, d = r.shape
    w = 2 * RWKV_HEAD
    row = pl.BlockSpec((None, tm, d), lambda b, p, i: (b, i, 0))
    vec = _full_spec((1, d))
    return pl.pallas_call(
        _rwkv_scan_body,
        grid=(bsz, 1, t // tm),
        in_specs=[row] * 8 + [vec, vec, vec, _full_spec(w_out.shape), _full_spec(e.shape), _full_spec(et.shape)],
        out_specs=row,
        out_shape=jax.ShapeDtypeStruct((bsz, t, d), F32),
        scratch_shapes=[pltpu.VMEM((d // w, w, w), F32), pltpu.VMEM((tm, d), F32)],
        compiler_params=_params(3),
        name="rwkv_scan",
    )(r, k, v, g, kk, bv, gate, h, r_k, gn_g, gn_b, w_out, e, et)


def _rwkv_out_tile(h_ref, y_ref, r_ref, k_ref, v_ref, gate_ref, rk_ref, gg_ref, gb_ref, wout_ref,
                   e_ref, et_ref, o_ref):
    tm = h_ref.shape[0]
    subs = [pl.ds(s0, SUB_ROWS) for s0 in range(0, tm, SUB_ROWS)]
    each = lambda f, *lists: [f(*xs) for xs in zip(*lists)]
    reduce_ = lambda z: _head_sums_reduce(z, e_ref[...])
    expand = lambda s: _head_sums_expand(s, et_ref[...])
    y = [y_ref[rows, :] for rows in subs]
    rk = [r_ref[rows, :] * k_ref[rows, :] * rk_ref[...] for rows in subs]
    y_sum, rk_sum = each(reduce_, y), each(reduce_, rk)
    y_sum, rk_sum = each(expand, y_sum), each(expand, rk_sum)
    dy = each(lambda a, s: a - s * (1.0 / RWKV_HEAD), y, y_sum)
    var = each(expand, each(lambda a: reduce_(a * a), dy))
    for rows, dy_, var_, rk_ in zip(subs, dy, var, rk_sum):
        yn = dy_ * lax.rsqrt(var_ * (1.0 / RWKV_HEAD) + RWKV_GN_EPS) * gg_ref[...] + gb_ref[...]
        out = (yn + rk_ * v_ref[rows, :]) * _silu(gate_ref[rows, :])
        o_ref[rows, :] = h_ref[rows, :] + _mm(out, wout_ref[...])


def _hgrn_body(h_ref, ng_ref, win_ref, lbl_ref, gg_ref, wout_ref, o_ref, state_ref, *, layer):
    tm, d = h_ref.shape
    c = CHUNK
    nh = d // HGRN_HEAD

    @pl.when(pl.program_id(1) == 0)
    def _():
        state_ref[...] = jnp.zeros_like(state_ref)

    logits = lbl_ref[...]
    ex = jnp.exp(logits - jnp.max(logits, axis=0, keepdims=True))
    lb = jnp.sum(ex[1:layer + 1, :], axis=0, keepdims=True) / jnp.sum(ex, axis=0, keepdims=True)

    nc = SUB_ROWS // c
    tri3 = jnp.concatenate([_tri_ones(c)] * 3, axis=1)
    row = _iota2((c, HGRN_HEAD), 0)
    top = row < HALF
    ri, ci_ = _iota2((c, c), 0), _iota2((c, c), 1)
    diag_blocks = (ri // HALF) == (ci_ // HALF)
    causal = ri >= ci_
    each = lambda f, *lists: [f(*xs) for xs in zip(*lists)]
    where = [(slice(ci * c, (ci + 1) * c), slice(hi * HGRN_HEAD, (hi + 1) * HGRN_HEAD))
             for ci in range(nc) for hi in range(nh)]
    take = lambda z: [z[rows, cols] for rows, cols in where]
    s_t = [state_ref[hi] for hi in range(nh)]

    subs = [pl.ds(s0, SUB_ROWS) for s0 in range(0, tm, SUB_ROWS)]
    xs = [h_ref[rows, :] for rows in subs]
    ps = [_mm(_rmsnorm(x, ng_ref[...]), win_ref[...]) for x in xs]
    for sub_rows, x, p in zip(subs, xs, ps):
        q = p[:, 0:d]
        f_pre = p[:, d:2 * d]
        v = p[:, 2 * d:3 * d]
        gate = p[:, 3 * d:4 * d]
        sig = _sigmoid(f_pre)
        log_f = jnp.log(lb + (1.0 - lb) * sig)
        k = (1.0 - lb) * (1.0 - sig)
        qs, ks, vs, gs = take(q), take(k), take(v), take(log_f)
        b = each(lambda x: _cumsum_rows_fused(tri3, x), gs)
        bl = each(lambda x: x[c - 1:c], b)
        beta = each(lambda x: x[HALF - 1:HALF], b)
        mid = each(lambda x: jnp.where(top, x[HALF // 2:HALF // 2 + 1], x[HALF + HALF // 2:HALF + HALF // 2 + 1]), b)
        q_d = each(lambda a, x, m: (a * jnp.exp(x - m)).astype(BF16), qs, b, mid)
        k_d = each(lambda a, x, m: (a * jnp.exp(m - x)).astype(BF16), ks, b, mid)
        q_o = each(lambda a, x, e: jnp.where(top, 0.0, a * jnp.exp(jnp.minimum(x - e, 0.0))).astype(BF16), qs, b, beta)
        k_o = each(lambda a, x, e: jnp.where(top, a * jnp.exp(jnp.minimum(e - x, 0.0)), 0.0).astype(BF16), ks, b, beta)
        q_in = each(lambda a, x: (a * jnp.exp(x)).astype(BF16), qs, b)
        k_out = each(lambda a, x, e: (a * jnp.exp(e - x)).astype(BF16), ks, b, bl)
        v_b = each(lambda a: a.astype(BF16), vs)
        att = each(lambda qd, kd, qo, ko: jnp.where(diag_blocks, jnp.where(causal, _mm_nt(qd, kd), 0.0), _mm_nt(qo, ko)),
                   q_d, k_d, q_o, k_o)
        o_intra = each(_mm, att, v_b)
        kv = each(_mm_tn, v_b, k_out)
        s_in = []
        for n in range(len(where)):
            hi = n % nh
            s_in.append(s_t[hi])
            s_t[hi] = s_t[hi] * jnp.exp(bl[n]) + kv[n]
        o = each(lambda oi, a, s: oi + _mm_nt(a, s), o_intra, q_in, s_in)
        o = each(lambda z: z * lax.rsqrt(jnp.mean(z * z, axis=-1, keepdims=True) + RMS_EPS), o)
        o_all = jnp.concatenate([jnp.concatenate(o[ci * nh:(ci + 1) * nh], axis=1) for ci in range(nc)], axis=0)
        o_all = o_all * gg_ref[...] * _silu(gate)
        o_ref[sub_rows, :] = x + _mm(o_all, wout_ref[...])
    for hi in range(nh):
        state_ref[hi] = s_t[hi]


def _hgrn_layer(h, ng, w_in, lb_logits, gn_g, w_out, *, layer, tm):
    bsz, t, d = h.shape
    row = _row_spec(tm, d)
    vec = _full_spec((1, d))
    return pl.pallas_call(
        functools.partial(_hgrn_body, layer=layer),
        grid=(bsz, t // tm),
        in_specs=[row, vec, _full_spec(w_in.shape), _full_spec(lb_logits.shape), vec, _full_spec(w_out.shape)],
        out_specs=row,
        out_shape=jax.ShapeDtypeStruct((bsz, t, d), F32),
        scratch_shapes=[pltpu.VMEM((d // HGRN_HEAD, HGRN_HEAD, HGRN_HEAD), F32)],
        compiler_params=_params(2),
        name="hgrn_layer",
    )(h, ng, w_in, lb_logits, gn_g, w_out)


def _conv_body(h_ref, ng_ref, win_ref, cw_ref, wout_ref, o_ref, carry_ref):
    tm, d = h_ref.shape

    @pl.when(pl.program_id(1) == 0)
    def _():
        carry_ref[...] = jnp.zeros_like(carry_ref)

    subs = [pl.ds(s0, SUB_ROWS) for s0 in range(0, tm, SUB_ROWS)]
    xs = [h_ref[rows, :] for rows in subs]
    ps = [_mm(_rmsnorm(x, ng_ref[...]), win_ref[...]) for x in xs]
    tail = carry_ref[...]
    for rows, x, p in zip(subs, xs, ps):
        b_gate = p[:, 0:d]
        y = p[:, d:2 * d] * p[:, 2 * d:3 * d]
        gate = p[:, 3 * d:4 * d]
        y1 = _shift_rows(y, tail, 1)
        y2 = _shift_rows(y, tail, 2)
        tail = y[SUB_ROWS - SUBLANES:SUB_ROWS, :]
        yc = cw_ref[0:1, :] * y2 + cw_ref[1:2, :] * y1 + cw_ref[2:3, :] * y
        o_ref[rows, :] = x + _mm(b_gate * yc * _silu(gate), wout_ref[...])
    carry_ref[...] = tail


def _conv_layer(h, ng, w_in, conv_w, w_out, *, tm):
    bsz, t, d = h.shape
    row = _row_spec(tm, d)
    return pl.pallas_call(
        _conv_body,
        grid=(bsz, t // tm),
        in_specs=[row, _full_spec((1, d)), _full_spec(w_in.shape), _full_spec(conv_w.shape), _full_spec(w_out.shape)],
        out_specs=row,
        out_shape=jax.ShapeDtypeStruct((bsz, t, d), F32),
        scratch_shapes=[pltpu.VMEM((SUBLANES, d), F32)],
        compiler_params=_params(2),
        name="conv_layer",
    )(h, ng, w_in, conv_w, w_out)


def _gmlp_body(h_ref, ng_ref, win_ref, vg_ref, ws_ref, bs_ref, wout_ref, fg_ref, o_ref):
    tm, d = h_ref.shape
    c = GMLP_CHUNK
    causal = _iota2((c, c), 0) >= _iota2((c, c), 1)
    bias = bs_ref[...]
    nc = SUB_ROWS // c
    subs = [pl.ds(s0, SUB_ROWS) for s0 in range(0, tm, SUB_ROWS)]
    xs = [h_ref[rows, :] for rows in subs]
    ps = [_mm(_rmsnorm(x, ng_ref[...]), win_ref[...]) for x in xs]
    for rows, x, p in zip(subs, xs, ps):
        u = p[:, 0:d]
        vb = _rmsnorm(p[:, d:2 * d], vg_ref[...]).astype(BF16)
        gate = p[:, 2 * d:3 * d]
        parts = []
        for gi in range(d // GMLP_GROUP):
            cols = slice(gi * GMLP_GROUP, (gi + 1) * GMLP_GROUP)
            v_g = jnp.concatenate([vb[ci * c:(ci + 1) * c, cols] for ci in range(nc)], axis=1)
            parts.append(_mm(jnp.where(causal, ws_ref[gi], 0.0), v_g))
        s = jnp.concatenate(
            [jnp.concatenate([q[:, ci * GMLP_GROUP:(ci + 1) * GMLP_GROUP] for q in parts], axis=1) + bias
             for ci in range(nc)], axis=0)
        out = x + _mm(u * s * _silu(gate), wout_ref[...])
        o_ref[rows, :] = _rmsnorm(out, fg_ref[...])


def _gmlp_layer(h, ng, w_in, v_g, w_s, bs_full, w_out, final_g, *, tm):
    bsz, t, d = h.shape
    row = _row_spec(tm, d)
    vec = _full_spec((1, d))
    return pl.pallas_call(
        _gmlp_body,
        grid=(bsz, t // tm),
        in_specs=[row, vec, _full_spec(w_in.shape), vec, _full_spec(w_s.shape), _full_spec(bs_full.shape),
                  _full_spec(w_out.shape), vec],
        out_specs=row,
        out_shape=jax.ShapeDtypeStruct((bsz, t, d), F32),
        compiler_params=_params(2),
        name="gmlp_layer",
    )(h, ng, w_in, v_g, w_s, bs_full, w_out, final_g)


def _rwkv_layer(h, ng, w_in, mu, w0, w_w2, a0, w_a2, k_k, k_a, r_k, gn_g, gn_b, w_out):
    d = h.shape[-1]
    nh = d // RWKV_HEAD
    zeros = jnp.zeros((RWKV_LORA, d), F32)
    ww_pad = jnp.concatenate([w_w2, zeros], axis=0).astype(BF16)
    wa_pad = jnp.concatenate([zeros, w_a2], axis=0).astype(BF16)
    e = (jnp.arange(d)[:, None] // RWKV_HEAD == jnp.arange(LANES)[None, :]).astype(BF16)
    et = jnp.concatenate([e.T, e.T], axis=0)
    vec = lambda z: z.reshape(1, -1)
    r, k, v, g, kk, bv, gate = _rwkv_prep(h, vec(ng), w_in.astype(BF16), vec(mu), vec(w0), ww_pad, vec(a0), wa_pad,
                                          vec(k_k), vec(k_a), e, et, tm=512)
    del nh
    return _rwkv_scan(h, r, k, v, g, kk, bv, gate, vec(r_k), vec(gn_g), vec(gn_b), w_out.astype(BF16), e, et, tm=256)


def kernel(x, norm_g, final_g, rwkv_w_in, rwkv_mu, rwkv_w0, rwkv_w_w2, rwkv_a0, rwkv_w_a2, rwkv_k_k, rwkv_k_a, rwkv_r_k, rwkv_gn_g, rwkv_gn_b, rwkv_w_out, hgrn_lb_logits, hgrn_w_in, hgrn_gn_g, hgrn_w_out, conv_w_in, conv_w, conv_w_out, gmlp_w_in, gmlp_v_g, gmlp_w_s, gmlp_b_s, gmlp_w_out):
    depth = norm_g.shape[0]
    assert depth == 4, "the fused final norm assumes the gMLP layer is the last one"
    vec = lambda z: z.reshape(1, -1)
    h = x
    h = _rwkv_layer(h, norm_g[0], rwkv_w_in[0], rwkv_mu[0], rwkv_w0[0], rwkv_w_w2[0], rwkv_a0[0], rwkv_w_a2[0],
                    rwkv_k_k[0], rwkv_k_a[0], rwkv_r_k[0], rwkv_gn_g[0], rwkv_gn_b[0], rwkv_w_out[0])
    h = _hgrn_layer(h, vec(norm_g[1]), hgrn_w_in[0].astype(BF16), hgrn_lb_logits, vec(hgrn_gn_g[0]),
                    hgrn_w_out[0].astype(BF16), layer=1, tm=512)
    h = _conv_layer(h, vec(norm_g[2]), conv_w_in[0].astype(BF16), conv_w[0], conv_w_out[0].astype(BF16), tm=512)
    bs_full = jnp.repeat(gmlp_b_s[0].T, GMLP_GROUP, axis=1)
    h = _gmlp_layer(h, vec(norm_g[3]), gmlp_w_in[0].astype(BF16), vec(gmlp_v_g[0]), gmlp_w_s[0].astype(BF16),
                    bs_full, gmlp_w_out[0].astype(BF16), vec(final_g), tm=512)
    return h
```

```python
import functools

import jax
import jax.numpy as jnp
from jax import lax
from jax.experimental import pallas as pl
from jax.experimental.pallas import tpu as pltpu

F32 = jnp.float32
BF16 = jnp.bfloat16

RMS_EPS = 1e-6
RWKV_HEAD = 64
RWKV_GN_EPS = 64e-5
RWKV_LORA = 64
DECAY_SCALE = 0.6065306597126334
HGRN_HEAD = 128
GMLP_CHUNK = 128
GMLP_GROUP = 128
CONV_WIDTH = 3

LANES = 128
SUBLANES = 8
CHUNK = 64
HALF = CHUNK // 2
SUB_ROWS = 256
SCAN_GROUP_CHUNKS = 2
VMEM_LIMIT = 56 * 1024 * 1024


def _mm(a, b):
    return jnp.dot(a.astype(BF16), b.astype(BF16), preferred_element_type=F32)


def _mm_nt(a, b):
    return lax.dot_general(a.astype(BF16), b.astype(BF16), (((1,), (1,)), ((), ())),
                           preferred_element_type=F32)


def _mm_tn(a, b):
    return lax.dot_general(a.astype(BF16), b.astype(BF16), (((0,), (0,)), ((), ())),
                           preferred_element_type=F32)


def _split2(x):
    hi = x.astype(BF16)
    lo = (x - hi.astype(F32)).astype(BF16)
    return hi, lo


def _head_sums(x, e, et2):
    return _head_sums_expand(_head_sums_reduce(x, e), et2)


def _head_sums_reduce(x, e):
    return jnp.dot(x.astype(BF16), e, preferred_element_type=F32)


def _head_sums_expand(s, et2):
    hi, lo = _split2(s)
    return jnp.dot(jnp.concatenate([hi, lo], axis=1), et2, preferred_element_type=F32)


def _cumsum_rows_fused(tri2, g):
    return jnp.dot(tri2, jnp.concatenate(_split2(g), axis=0), preferred_element_type=F32)


def _sigmoid(x):
    return 0.5 * jnp.tanh(0.5 * x) + 0.5


def _silu(x):
    return x * _sigmoid(x)


def _log1pexp_neg(z):
    return jnp.log(1.0 + jnp.exp(-z))


def _softplus(x):
    return jnp.maximum(x, 0.0) + _log1pexp_neg(jnp.abs(x))


def _rmsnorm(x, g):
    ms = jnp.mean(x * x, axis=-1, keepdims=True)
    return x * lax.rsqrt(ms + RMS_EPS) * g


def _iota2(shape, axis):
    return lax.broadcasted_iota(jnp.int32, shape, axis)


def _shift_rows(x, tail, n):
    rolled = pltpu.roll(x, n, 0)
    head = rolled[0:SUBLANES]
    row = _iota2(head.shape, 0)
    head = jnp.where(row < n, pltpu.roll(tail, n, 0), head)
    return jnp.concatenate([head, rolled[SUBLANES:]], axis=0)


def _tri_ones(n):
    return (_iota2((n, n), 0) >= _iota2((n, n), 1)).astype(BF16)


def _row_spec(tm, d):
    return pl.BlockSpec((None, tm, d), lambda b, t: (b, t, 0))


def _full_spec(shape):
    n = len(shape)
    return pl.BlockSpec(shape, lambda *_: (0,) * n)


def _params(n_axes):
    return pltpu.CompilerParams(dimension_semantics=("arbitrary",) * n_axes, vmem_limit_bytes=VMEM_LIMIT)


def _rwkv_prep_body(h_ref, ng_ref, win_ref, mu_ref, w0_ref, ww_ref, a0_ref, wa_ref, kk_ref, ka_ref,
                    e_ref, et_ref,
                    r_out, k_out, v_out, g_out, kk_out, bv_out, gate_out, carry_ref, *, d):
    tm = h_ref.shape[0]

    @pl.when(pl.program_id(1) == 0)
    def _():
        carry_ref[...] = jnp.zeros_like(carry_ref)

    for s0 in range(0, tm, SUB_ROWS):
        rows = pl.ds(s0, SUB_ROWS)
        hn = _rmsnorm(h_ref[rows, :], ng_ref[...]).astype(BF16)

        def project(lo, hi):
            p = jnp.dot(hn, win_ref[:, lo:hi], preferred_element_type=F32)
            prev = _shift_rows(p, carry_ref[:, lo:hi], 1)
            carry_ref[:, lo:hi] = p[SUB_ROWS - SUBLANES:SUB_ROWS, :]
            return p + (prev - p) * mu_ref[:, lo:hi]

        lw = project(3 * d, 3 * d + 2 * RWKV_LORA)
        k = project(d, 2 * d)
        lw = jnp.where(_iota2(lw.shape, 1) < RWKV_LORA, jnp.tanh(lw), lw)
        dw = _mm(lw, ww_ref[...])
        da = _mm(lw, wa_ref[...])
        r_out[rows, :] = project(0, d)
        kk = k * kk_ref[...]
        n2 = _head_sums_reduce(kk * kk, e_ref[...])
        v_out[rows, :] = project(2 * d, 3 * d)
        n2 = _head_sums_expand(n2, et_ref[...])
        gate_out[rows, :] = project(3 * d + 2 * RWKV_LORA, win_ref.shape[1])
        a = _sigmoid(a0_ref[...] + da)
        kk = kk * lax.rsqrt(jnp.maximum(n2, 1e-24))
        k_out[rows, :] = k * (1.0 + (a - 1.0) * ka_ref[...])
        g_out[rows, :] = -DECAY_SCALE * _sigmoid(w0_ref[...] + dw)
        kk_out[rows, :] = kk
        bv_out[rows, :] = kk * a


def _rwkv_prep(h, ng, w_in, mu, w0, ww_pad, a0, wa_pad, k_k, k_a, e, et, *, tm):
    bsz, t, d = h.shape
    cols = w_in.shape[1]
    row = _row_spec(tm, d)
    vec = _full_spec((1, d))
    out = jax.ShapeDtypeStruct((bsz, t, d), F32)
    return pl.pallas_call(
        functools.partial(_rwkv_prep_body, d=d),
        grid=(bsz, t // tm),
        in_specs=[row, vec, _full_spec(w_in.shape), _full_spec((1, cols)), vec, _full_spec(ww_pad.shape), vec,
                  _full_spec(wa_pad.shape), vec, vec, _full_spec(e.shape), _full_spec(et.shape)],
        out_specs=[row] * 7,
        out_shape=[out] * 7,
        scratch_shapes=[pltpu.VMEM((SUBLANES, cols), F32)],
        compiler_params=_params(2),
        name="rwkv_prep",
    )(h, ng, w_in, mu, w0, ww_pad, a0, wa_pad, k_k, k_a, e, et)


def _stack_heads(x, m0):
    return jnp.concatenate([jnp.where(m0, x, 0.0), jnp.where(m0, 0.0, x)], axis=0)


def _unit_lower_inverse(lows, i2, j2):
    s = 1
    eye = jnp.where(i2 == j2, 1.0, 0.0)
    t_inv = [eye for _ in lows]
    while s < CHUNK:
        keep = ((i2 // (2 * s)) == (j2 // (2 * s))) & ((i2 // s) != (j2 // s))
        subs = [jnp.where(keep, low, 0.0).astype(BF16) for low in lows]
        if s == 1:
            t_inv = [t + sub for t, sub in zip(t_inv, subs)]
        else:
            t_bf = [t.astype(BF16) for t in t_inv]
            left = [_mm(tb, sub) for tb, sub in zip(t_bf, subs)]
            t_inv = [t + _mm(lt, tb) for t, lt, tb in zip(t_inv, left, t_bf)]
        s *= 2
    return t_inv


def _rwkv_scan_body(r_ref, k_ref, v_ref, g_ref, kk_ref, bv_ref, gate_ref, h_ref, rk_ref, gg_ref, gb_ref,
                    wout_ref, e_ref, et_ref, o_ref, state_ref, y_ref):
    tm = r_ref.shape[0]
    c = CHUNK
    w = 2 * RWKV_HEAD
    nc = tm // c
    n_pairs = r_ref.shape[1] // w

    @pl.when(pl.program_id(2) == 0)
    def _():
        state_ref[...] = jnp.zeros_like(state_ref)

    m0 = _iota2((c, w), 1) < RWKV_HEAD
    m0b = m0
    tri2 = jnp.concatenate([_tri_ones(c)] * 2, axis=1)
    i2 = _iota2((2 * c, w), 0)
    j2 = _iota2((2 * c, w), 1)
    same_head = (i2 // RWKV_HEAD) == (j2 // RWKV_HEAD)
    ir = _iota2((c, w), 0)
    jr = _iota2((c, w), 1) % c
    ir2 = _iota2((c, 2 * w), 0)
    jr2 = _iota2((c, 2 * w), 1) % c
    stack = functools.partial(_stack_heads, m0=m0)
    each = lambda f, *lists: [f(*xs) for xs in zip(*lists)]
    h_state = [state_ref[pi] for pi in range(n_pairs)]
    for c0 in range(0, nc, SCAN_GROUP_CHUNKS):
        where = [(pl.ds(ci * c, c), pl.ds(pi * w, w)) for ci in range(c0, min(c0 + SCAN_GROUP_CHUNKS, nc))
                 for pi in range(n_pairs)]
        load = lambda ref: [ref[rows, cols] for rows, cols in where]

        r, k, v, g, kk, bv = load(r_ref), load(k_ref), load(v_ref), load(g_ref), load(kk_ref), load(bv_ref)
        bc = each(lambda x: _cumsum_rows_fused(tri2, x), g)
        bl = each(lambda x: x[c - 1:c, :], bc)
        p_inv = each(lambda x: jnp.exp(-x), bc)
        p_end = each(lambda x, y: jnp.exp(x - y), bl, bc)
        rt = each(lambda a, x: a * jnp.exp(x), r, bc)
        at2 = each(lambda a, x, y: stack(-a * jnp.exp(x - y)).astype(BF16), kk, bc, g)
        v2 = each(lambda a: stack(a).astype(BF16), v)
        lhs = each(lambda a, x: jnp.concatenate([a, x.astype(BF16)], axis=0), at2, rt)
        rhs = each(lambda a, x, p: jnp.concatenate([stack(a * p), stack(x * p)], axis=0).astype(BF16), bv, k, p_inv)
        end = each(lambda a, x, p: jnp.concatenate([a * p, x * p], axis=0).astype(BF16), bv, k, p_end)

        prod = each(_mm_nt, lhs, rhs)
        low = each(lambda x: jnp.where(j2 < i2, x[0:2 * c, 0:w], 0.0), prod)
        a_ak = each(lambda x: jnp.where(jr < ir, x[0:c, w:2 * w] + x[c:2 * c, w:2 * w], 0.0).astype(BF16), prod)
        a_r = each(lambda x: jnp.where(jr2 <= ir2, x[2 * c:3 * c, :], 0.0).astype(BF16), prod)
        t_inv = _unit_lower_inverse(low, i2, j2)
        t_row = each(lambda t: (t[0:c, :] + t[c:2 * c, :]).astype(BF16), t_inv)
        z = each(_mm, a_ak, v2)
        w12 = each(lambda t, a, x: _mm(t, jnp.concatenate([a, stack(x).astype(BF16)], axis=1)), t_row, at2, z)
        w12b = each(lambda x: x.astype(BF16), w12)
        q12 = each(lambda a, x: _mm(a[:, 0:w], jnp.concatenate(
            [_stack_heads(x[:, 0:w], m0b), _stack_heads(x[:, w:2 * w], m0b)], axis=1)), a_r, w12b)
        q1 = each(lambda a, x: a + x[:, 0:w], rt, q12)
        q2 = each(lambda x, a, y: x[:, w:2 * w] + _mm(a[:, w:2 * w], y), q12, a_r, v2)
        zero_c = jnp.zeros((c, w), BF16)
        ti = each(lambda e, x, y: _mm_tn(e, jnp.concatenate(
            [x, jnp.concatenate([zero_c, y.astype(BF16)], axis=1)], axis=0)), end, w12b, v)
        trans = each(lambda x, e: jnp.where(same_head, x[:, 0:w], 0.0)
                     + jnp.where(i2 == j2, jnp.broadcast_to(jnp.exp(e), (2 * c, w)), 0.0), ti, bl)
        inject = each(lambda x: jnp.where(same_head, x[:, w:2 * w], 0.0), ti)
        q1t = each(lambda a, x: jnp.concatenate([a, x], axis=0).astype(BF16), q1, trans)

        for n, (rows, cols) in enumerate(where):
            pi = n % n_pairs
            yh = _mm(q1t[n], h_state[pi])
            y_ref[rows, cols] = yh[0:c, :] + q2[n]
            h_state[pi] = yh[c:3 * c, :] + inject[n]
    for pi in range(n_pairs):
        state_ref[pi] = h_state[pi]
    _rwkv_out_tile(h_ref, y_ref, r_ref, k_ref, v_ref, gate_ref, rk_ref, gg_ref, gb_ref, wout_ref, e_ref, et_ref, o_ref)


def _rwkv_scan(h, r, k, v, g, kk, bv, gate, r_k, gn_g, gn_b, w_out, e, et, *, tm):
    bsz, t, d = r.shape
    w = 2 * RWKV_HEAD
    row = pl.BlockSpec((None, tm, d), lambda b, p, i: (b, i, 0))
    vec = _full_spec((1, d))
    return pl.pallas_call(
        _rwkv_scan_body,
        grid=(bsz, 1, t // tm),
        in_specs=[row] * 8 + [vec, vec, vec, _full_spec(w_out.shape), _full_spec(e.shape), _full_spec(et.shape)],
        out_specs=row,
        out_shape=jax.ShapeDtypeStruct((bsz, t, d), F32),
        scratch_shapes=[pltpu.VMEM((d // w, w, w), F32), pltpu.VMEM((tm, d), F32)],
        compiler_params=_params(3),
        name="rwkv_scan",
    )(r, k, v, g, kk, bv, gate, h, r_k, gn_g, gn_b, w_out, e, et)


def _rwkv_out_tile(h_ref, y_ref, r_ref, k_ref, v_ref, gate_ref, rk_ref, gg_ref, gb_ref, wout_ref,
                   e_ref, et_ref, o_ref):
    tm = h_ref.shape[0]
    subs = [pl.ds(s0, SUB_ROWS) for s0 in range(0, tm, SUB_ROWS)]
    each = lambda f, *lists: [f(*xs) for xs in zip(*lists)]
    reduce_ = lambda z: _head_sums_reduce(z, e_ref[...])
    expand = lambda s: _head_sums_expand(s, et_ref[...])
    y = [y_ref[rows, :] for rows in subs]
    rk = [r_ref[rows, :] * k_ref[rows, :] * rk_ref[...] for rows in subs]
    y_sum, rk_sum = each(reduce_, y), each(reduce_, rk)
    y_sum, rk_sum = each(expand, y_sum), each(expand, rk_sum)
    dy = each(lambda a, s: a - s * (1.0 / RWKV_HEAD), y, y_sum)
    var = each(expand, each(lambda a: reduce_(a * a), dy))
    for rows, dy_, var_, rk_ in zip(subs, dy, var, rk_sum):
        yn = dy_ * lax.rsqrt(var_ * (1.0 / RWKV_HEAD) + RWKV_GN_EPS) * gg_ref[...] + gb_ref[...]
        out = (yn + rk_ * v_ref[rows, :]) * _silu(gate_ref[rows, :])
        o_ref[rows, :] = h_ref[rows, :] + _mm(out, wout_ref[...])


def _hgrn_body(h_ref, ng_ref, win_ref, lbl_ref, gg_ref, wout_ref, o_ref, state_ref, *, layer):
    tm, d = h_ref.shape
    c = CHUNK
    nh = d // HGRN_HEAD

    @pl.when(pl.program_id(1) == 0)
    def _():
        state_ref[...] = jnp.zeros_like(state_ref)

    logits = lbl_ref[...]
    ex = jnp.exp(logits - jnp.max(logits, axis=0, keepdims=True))
    lb = jnp.sum(ex[1:layer + 1, :], axis=0, keepdims=True) / jnp.sum(ex, axis=0, keepdims=True)

    nc = SUB_ROWS // c
    tri2 = jnp.concatenate([_tri_ones(c)] * 2, axis=1)
    row = _iota2((c, HGRN_HEAD), 0)
    top = row < HALF
    ri, ci_ = _iota2((c, c), 0), _iota2((c, c), 1)
    diag_blocks = (ri // HALF) == (ci_ // HALF)
    causal = ri >= ci_
    each = lambda f, *lists: [f(*xs) for xs in zip(*lists)]
    where = [(slice(ci * c, (ci + 1) * c), slice(hi * HGRN_HEAD, (hi + 1) * HGRN_HEAD))
             for ci in range(nc) for hi in range(nh)]
    take = lambda z: [z[rows, cols] for rows, cols in where]
    s_t = [state_ref[hi] for hi in range(nh)]

    subs = [pl.ds(s0, SUB_ROWS) for s0 in range(0, tm, SUB_ROWS)]
    xs = [h_ref[rows, :] for rows in subs]
    ps = [_mm(_rmsnorm(x, ng_ref[...]), win_ref[...]) for x in xs]
    for sub_rows, x, p in zip(subs, xs, ps):
        q = p[:, 0:d]
        f_pre = p[:, d:2 * d]
        v = p[:, 2 * d:3 * d]
        gate = p[:, 3 * d:4 * d]
        sig = _sigmoid(f_pre)
        log_f = jnp.log(lb + (1.0 - lb) * sig)
        k = (1.0 - lb) * (1.0 - sig)
        qs, ks, vs, gs = take(q), take(k), take(v), take(log_f)
        b = each(lambda x: _cumsum_rows_fused(tri2, x), gs)
        bl = each(lambda x: x[c - 1:c], b)
        beta = each(lambda x: x[HALF - 1:HALF], b)
        m_a = each(lambda x: x[HALF // 2:HALF // 2 + 1], b)
        m_b = each(lambda x: x[HALF + HALF // 2:HALF + HALF // 2 + 1], b)
        mid = each(lambda x, y: jnp.where(top, x, y), m_a, m_b)
        q_df = each(lambda a, x, m: a * jnp.exp(x - m), qs, b, mid)
        k_df = each(lambda a, x, m: a * jnp.exp(m - x), ks, b, mid)
        q_d = each(lambda a: a.astype(BF16), q_df)
        k_d = each(lambda a: a.astype(BF16), k_df)
        q_o = each(lambda a, y, e: jnp.where(top, 0.0, a * jnp.exp(y - e)).astype(BF16), q_df, m_b, beta)
        k_o = each(lambda a, x, e: jnp.where(top, a * jnp.exp(e - x), 0.0).astype(BF16), k_df, m_a, beta)
        q_in = each(lambda a, x, y: (a * jnp.where(top, jnp.exp(x), jnp.exp(y))).astype(BF16),
                    q_df, m_a, m_b)
        k_out = each(lambda a, x, y, e: (a * jnp.where(top, jnp.exp(e - x), jnp.exp(e - y))).astype(BF16),
                     k_df, m_a, m_b, bl)
        v_b = each(lambda a: a.astype(BF16), vs)
        att = each(lambda qd, kd, qo, ko: jnp.where(diag_blocks, jnp.where(causal, _mm_nt(qd, kd), 0.0), _mm_nt(qo, ko)),
                   q_d, k_d, q_o, k_o)
        o_intra = each(_mm, att, v_b)
        kv = each(_mm_tn, v_b, k_out)
        s_in = []
        for n in range(len(where)):
            hi = n % nh
            s_in.append(s_t[hi])
            s_t[hi] = s_t[hi] * jnp.exp(bl[n]) + kv[n]
        o = each(lambda oi, a, s: oi + _mm_nt(a, s), o_intra, q_in, s_in)
        o = each(lambda z: z * lax.rsqrt(jnp.mean(z * z, axis=-1, keepdims=True) + RMS_EPS), o)
        o_all = jnp.concatenate([jnp.concatenate(o[ci * nh:(ci + 1) * nh], axis=1) for ci in range(nc)], axis=0)
        o_all = o_all * gg_ref[...] * _silu(gate)
        o_ref[sub_rows, :] = x + _mm(o_all, wout_ref[...])
    for hi in range(nh):
        state_ref[hi] = s_t[hi]


def _hgrn_layer(h, ng, w_in, lb_logits, gn_g, w_out, *, layer, tm):
    bsz, t, d = h.shape
    row = _row_spec(tm, d)
    vec = _full_spec((1, d))
    return pl.pallas_call(
        functools.partial(_hgrn_body, layer=layer),
        grid=(bsz, t // tm),
        in_specs=[row, vec, _full_spec(w_in.shape), _full_spec(lb_logits.shape), vec, _full_spec(w_out.shape)],
        out_specs=row,
        out_shape=jax.ShapeDtypeStruct((bsz, t, d), F32),
        scratch_shapes=[pltpu.VMEM((d // HGRN_HEAD, HGRN_HEAD, HGRN_HEAD), F32)],
        compiler_params=_params(2),
        name="hgrn_layer",
    )(h, ng, w_in, lb_logits, gn_g, w_out)


def _conv_body(h_ref, ng_ref, win_ref, cw_ref, wout_ref, o_ref, carry_ref):
    tm, d = h_ref.shape

    @pl.when(pl.program_id(1) == 0)
    def _():
        carry_ref[...] = jnp.zeros_like(carry_ref)

    subs = [pl.ds(s0, SUB_ROWS) for s0 in range(0, tm, SUB_ROWS)]
    xs = [h_ref[rows, :] for rows in subs]
    ps = [_mm(_rmsnorm(x, ng_ref[...]), win_ref[...]) for x in xs]
    tail = carry_ref[...]
    for rows, x, p in zip(subs, xs, ps):
        b_gate = p[:, 0:d]
        y = p[:, d:2 * d] * p[:, 2 * d:3 * d]
        gate = p[:, 3 * d:4 * d]
        y1 = _shift_rows(y, tail, 1)
        y2 = _shift_rows(y, tail, 2)
        tail = y[SUB_ROWS - SUBLANES:SUB_ROWS, :]
        yc = cw_ref[0:1, :] * y2 + cw_ref[1:2, :] * y1 + cw_ref[2:3, :] * y
        o_ref[rows, :] = x + _mm(b_gate * yc * _silu(gate), wout_ref[...])
    carry_ref[...] = tail


def _conv_layer(h, ng, w_in, conv_w, w_out, *, tm):
    bsz, t, d = h.shape
    row = _row_spec(tm, d)
    return pl.pallas_call(
        _conv_body,
        grid=(bsz, t // tm),
        in_specs=[row, _full_spec((1, d)), _full_spec(w_in.shape), _full_spec(conv_w.shape), _full_spec(w_out.shape)],
        out_specs=row,
        out_shape=jax.ShapeDtypeStruct((bsz, t, d), F32),
        scratch_shapes=[pltpu.VMEM((SUBLANES, d), F32)],
        compiler_params=_params(2),
        name="conv_layer",
    )(h, ng, w_in, conv_w, w_out)


def _gmlp_body(h_ref, ng_ref, win_ref, vg_ref, ws_ref, bs_ref, wout_ref, fg_ref, o_ref):
    tm, d = h_ref.shape
    c = GMLP_CHUNK
    causal = _iota2((c, c), 0) >= _iota2((c, c), 1)
    bias = bs_ref[...]
    nc = SUB_ROWS // c
    subs = [pl.ds(s0, SUB_ROWS) for s0 in range(0, tm, SUB_ROWS)]
    xs = [h_ref[rows, :] for rows in subs]
    ps = [_mm(_rmsnorm(x, ng_ref[...]), win_ref[...]) for x in xs]
    for rows, x, p in zip(subs, xs, ps):
        u = p[:, 0:d]
        vb = _rmsnorm(p[:, d:2 * d], vg_ref[...]).astype(BF16)
        gate = p[:, 2 * d:3 * d]
        parts = []
        for gi in range(d // GMLP_GROUP):
            cols = slice(gi * GMLP_GROUP, (gi + 1) * GMLP_GROUP)
            v_g = jnp.concatenate([vb[ci * c:(ci + 1) * c, cols] for ci in range(nc)], axis=1)
            parts.append(_mm(jnp.where(causal, ws_ref[gi], 0.0), v_g))
        s = jnp.concatenate(
            [jnp.concatenate([q[:, ci * GMLP_GROUP:(ci + 1) * GMLP_GROUP] for q in parts], axis=1) + bias
             for ci in range(nc)], axis=0)
        out = x + _mm(u * s * _silu(gate), wout_ref[...])
        o_ref[rows, :] = _rmsnorm(out, fg_ref[...])


def _gmlp_layer(h, ng, w_in, v_g, w_s, bs_full, w_out, final_g, *, tm):
    bsz, t, d = h.shape
    row = _row_spec(tm, d)
    vec = _full_spec((1, d))
    return pl.pallas_call(
        _gmlp_body,
        grid=(bsz, t // tm),
        in_specs=[row, vec, _full_spec(w_in.shape), vec, _full_spec(w_s.shape), _full_spec(bs_full.shape),
                  _full_spec(w_out.shape), vec],
        out_specs=row,
        out_shape=jax.ShapeDtypeStruct((bsz, t, d), F32),
        compiler_params=_params(2),
        name="gmlp_layer",
    )(h, ng, w_in, v_g, w_s, bs_full, w_out, final_g)


def _rwkv_layer(h, ng, w_in, mu, w0, w_w2, a0, w_a2, k_k, k_a, r_k, gn_g, gn_b, w_out):
    d = h.shape[-1]
    nh = d // RWKV_HEAD
    zeros = jnp.zeros((RWKV_LORA, d), F32)
    ww_pad = jnp.concatenate([w_w2, zeros], axis=0).astype(BF16)
    wa_pad = jnp.concatenate([zeros, w_a2], axis=0).astype(BF16)
    e = (jnp.arange(d)[:, None] // RWKV_HEAD == jnp.arange(LANES)[None, :]).astype(BF16)
    et = jnp.concatenate([e.T, e.T], axis=0)
    vec = lambda z: z.reshape(1, -1)
    r, k, v, g, kk, bv, gate = _rwkv_prep(h, vec(ng), w_in.astype(BF16), vec(mu), vec(w0), ww_pad, vec(a0), wa_pad,
                                          vec(k_k), vec(k_a), e, et, tm=512)
    del nh
    return _rwkv_scan(h, r, k, v, g, kk, bv, gate, vec(r_k), vec(gn_g), vec(gn_b), w_out.astype(BF16), e, et, tm=256)


def kernel(x, norm_g, final_g, rwkv_w_in, rwkv_mu, rwkv_w0, rwkv_w_w2, rwkv_a0, rwkv_w_a2, rwkv_k_k, rwkv_k_a, rwkv_r_k, rwkv_gn_g, rwkv_gn_b, rwkv_w_out, hgrn_lb_logits, hgrn_w_in, hgrn_gn_g, hgrn_w_out, conv_w_in, conv_w, conv_w_out, gmlp_w_in, gmlp_v_g, gmlp_w_s, gmlp_b_s, gmlp_w_out):
    depth = norm_g.shape[0]
    assert depth == 4, "the fused final norm assumes the gMLP layer is the last one"
    vec = lambda z: z.reshape(1, -1)
    h = x
    h = _rwkv_layer(h, norm_g[0], rwkv_w_in[0], rwkv_mu[0], rwkv_w0[0], rwkv_w_w2[0], rwkv_a0[0], rwkv_w_a2[0],
                    rwkv_k_k[0], rwkv_k_a[0], rwkv_r_k[0], rwkv_gn_g[0], rwkv_gn_b[0], rwkv_w_out[0])
    h = _hgrn_layer(h, vec(norm_g[1]), hgrn_w_in[0].astype(BF16), hgrn_lb_logits, vec(hgrn_gn_g[0]),
                    hgrn_w_out[0].astype(BF16), layer=1, tm=512)
    h = _conv_layer(h, vec(norm_g[2]), conv_w_in[0].astype(BF16), conv_w[0], conv_w_out[0].astype(BF16), tm=512)
    bs_full = jnp.repeat(gmlp_b_s[0].T, GMLP_GROUP, axis=1)
    h = _gmlp_layer(h, vec(norm_g[3]), gmlp_w_in[0].astype(BF16), vec(gmlp_v_g[0]), gmlp_w_s[0].astype(BF16),
                    bs_full, gmlp_w_out[0].astype(BF16), vec(final_g), tm=512)
    return h
```

```python
import functools

import jax
import jax.numpy as jnp
from jax import lax
from jax.experimental import pallas as pl
from jax.experimental.pallas import tpu as pltpu

F32 = jnp.float32
BF16 = jnp.bfloat16

RMS_EPS = 1e-6
RWKV_HEAD = 64
RWKV_GN_EPS = 64e-5
RWKV_LORA = 64
DECAY_SCALE = 0.6065306597126334
HGRN_HEAD = 128
GMLP_CHUNK = 128
GMLP_GROUP = 128
CONV_WIDTH = 3

LANES = 128
SUBLANES = 8
V7X_VMEM_BYTES = 64 * 1024 * 1024
VMEM_LIMIT = V7X_VMEM_BYTES * 7 // 8

CHUNK = 64
HALF = CHUNK // 2
SUB_ROWS = 256
SCAN_GROUP_CHUNKS = 2


def _tile_rows(name):
    return {"rwkv_prep": 2 * SUB_ROWS, "rwkv_scan": SUB_ROWS, "hgrn": 4 * SUB_ROWS, "conv": 4 * SUB_ROWS,
            "gmlp": 4 * SUB_ROWS}[name]


def _mm(a, b):
    return jnp.dot(a.astype(BF16), b.astype(BF16), preferred_element_type=F32)


def _mm_nt(a, b):
    return lax.dot_general(a.astype(BF16), b.astype(BF16), (((1,), (1,)), ((), ())),
                           preferred_element_type=F32)


def _mm_tn(a, b):
    return lax.dot_general(a.astype(BF16), b.astype(BF16), (((0,), (0,)), ((), ())),
                           preferred_element_type=F32)


def _split2(x):
    hi = x.astype(BF16)
    lo = (x - hi.astype(F32)).astype(BF16)
    return hi, lo


def _head_sums(x, e, et2):
    return _head_sums_expand(_head_sums_reduce(x, e), et2)


def _head_sums_reduce(x, e):
    return jnp.dot(x.astype(BF16), e, preferred_element_type=F32)


def _head_sums_expand(s, et2):
    hi, lo = _split2(s)
    return jnp.dot(jnp.concatenate([hi, lo], axis=1), et2, preferred_element_type=F32)


def _cumsum_rows_fused(tri2, g):
    return jnp.dot(tri2, jnp.concatenate(_split2(g), axis=0), preferred_element_type=F32)


def _sigmoid(x):
    return 0.5 * jnp.tanh(0.5 * x) + 0.5


def _silu(x):
    return x * _sigmoid(x)


def _log1pexp_neg(z):
    return jnp.log(1.0 + jnp.exp(-z))


def _softplus(x):
    return jnp.maximum(x, 0.0) + _log1pexp_neg(jnp.abs(x))


def _rmsnorm(x, g):
    ms = jnp.mean(x * x, axis=-1, keepdims=True)
    return x * lax.rsqrt(ms + RMS_EPS) * g


def _iota2(shape, axis):
    return lax.broadcasted_iota(jnp.int32, shape, axis)


def _shift_rows(x, tail, n):
    rolled = pltpu.roll(x, n, 0)
    head = rolled[0:SUBLANES]
    row = _iota2(head.shape, 0)
    head = jnp.where(row < n, pltpu.roll(tail, n, 0), head)
    return jnp.concatenate([head, rolled[SUBLANES:]], axis=0)


def _tri_ones(n):
    return (_iota2((n, n), 0) >= _iota2((n, n), 1)).astype(BF16)


def _projected_ahead(h_ref, tm, project):
    subs = [pl.ds(s0, SUB_ROWS) for s0 in range(0, tm, SUB_ROWS)]
    x = h_ref[subs[0], :]
    ahead = (x, project(x))
    for i, rows in enumerate(subs):
        cur = ahead
        if i + 1 < len(subs):
            x = h_ref[subs[i + 1], :]
            ahead = (x, project(x))
        yield rows, cur[0], cur[1]


def _row_spec(tm, d):
    return pl.BlockSpec((None, tm, d), lambda b, t: (b, t, 0))


def _full_spec(shape):
    n = len(shape)
    return pl.BlockSpec(shape, lambda *_: (0,) * n)


def _params(n_axes):
    return pltpu.CompilerParams(dimension_semantics=("arbitrary",) * n_axes, vmem_limit_bytes=VMEM_LIMIT)


def _rwkv_prep_body(h_ref, ng_ref, win_ref, mu_ref, w0_ref, ww_ref, a0_ref, wa_ref, kk_ref, ka_ref,
                    e_ref, et_ref,
                    r_out, k_out, v_out, g_out, kk_out, bv_out, gate_out, carry_ref, *, d):
    tm = h_ref.shape[0]

    @pl.when(pl.program_id(1) == 0)
    def _():
        carry_ref[...] = jnp.zeros_like(carry_ref)

    for s0 in range(0, tm, SUB_ROWS):
        rows = pl.ds(s0, SUB_ROWS)
        hn = _rmsnorm(h_ref[rows, :], ng_ref[...]).astype(BF16)

        def project(lo, hi):
            p = jnp.dot(hn, win_ref[:, lo:hi], preferred_element_type=F32)
            prev = _shift_rows(p, carry_ref[:, lo:hi], 1)
            carry_ref[:, lo:hi] = p[SUB_ROWS - SUBLANES:SUB_ROWS, :]
            return p + (prev - p) * mu_ref[:, lo:hi]

        lw = project(3 * d, 3 * d + 2 * RWKV_LORA)
        k = project(d, 2 * d)
        lw = jnp.where(_iota2(lw.shape, 1) < RWKV_LORA, jnp.tanh(lw), lw)
        dw = _mm(lw, ww_ref[...])
        da = _mm(lw, wa_ref[...])
        r_out[rows, :] = project(0, d)
        kk = k * kk_ref[...]
        n2 = _head_sums_reduce(kk * kk, e_ref[...])
        v_out[rows, :] = project(2 * d, 3 * d)
        n2 = _head_sums_expand(n2, et_ref[...])
        gate_out[rows, :] = project(3 * d + 2 * RWKV_LORA, win_ref.shape[1])
        a = _sigmoid(a0_ref[...] + da)
        kk = kk * lax.rsqrt(jnp.maximum(n2, 1e-24))
        k_out[rows, :] = k * (1.0 + (a - 1.0) * ka_ref[...])
        g_out[rows, :] = -DECAY_SCALE * _sigmoid(w0_ref[...] + dw)
        kk_out[rows, :] = kk
        bv_out[rows, :] = kk * a


def _rwkv_prep(h, ng, w_in, mu, w0, ww_pad, a0, wa_pad, k_k, k_a, e, et, *, tm):
    bsz, t, d = h.shape
    cols = w_in.shape[1]
    row = _row_spec(tm, d)
    vec = _full_spec((1, d))
    out = jax.ShapeDtypeStruct((bsz, t, d), F32)
    return pl.pallas_call(
        functools.partial(_rwkv_prep_body, d=d),
        grid=(bsz, t // tm),
        in_specs=[row, vec, _full_spec(w_in.shape), _full_spec((1, cols)), vec, _full_spec(ww_pad.shape), vec,
                  _full_spec(wa_pad.shape), vec, vec, _full_spec(e.shape), _full_spec(et.shape)],
        out_specs=[row] * 7,
        out_shape=[out] * 7,
        scratch_shapes=[pltpu.VMEM((SUBLANES, cols), F32)],
        compiler_params=_params(2),
        name="rwkv_prep",
    )(h, ng, w_in, mu, w0, ww_pad, a0, wa_pad, k_k, k_a, e, et)


def _stack_heads(x, m0):
    return jnp.concatenate([jnp.where(m0, x, 0.0), jnp.where(m0, 0.0, x)], axis=0)


def _unit_lower_inverse(lows, i2, j2):
    s = 1
    eye = jnp.where(i2 == j2, 1.0, 0.0)
    t_inv = [eye for _ in lows]
    while s < CHUNK:
        keep = ((i2 // (2 * s)) == (j2 // (2 * s))) & ((i2 // s) != (j2 // s))
        subs = [jnp.where(keep, low, 0.0).astype(BF16) for low in lows]
        if s == 1:
            t_inv = [t + sub for t, sub in zip(t_inv, subs)]
        else:
            t_bf = [t.astype(BF16) for t in t_inv]
            left = [_mm(tb, sub) for tb, sub in zip(t_bf, subs)]
            t_inv = [t + _mm(lt, tb) for t, lt, tb in zip(t_inv, left, t_bf)]
        s *= 2
    return t_inv


def _rwkv_scan_body(r_ref, k_ref, v_ref, g_ref, kk_ref, bv_ref, gate_ref, h_ref, rk_ref, gg_ref, gb_ref,
                    wout_ref, e_ref, et_ref, o_ref, state_ref, y_ref):
    tm = r_ref.shape[0]
    c = CHUNK
    w = 2 * RWKV_HEAD
    nc = tm // c
    n_pairs = r_ref.shape[1] // w

    @pl.when(pl.program_id(2) == 0)
    def _():
        state_ref[...] = jnp.zeros_like(state_ref)

    m0 = _iota2((c, w), 1) < RWKV_HEAD
    tri2 = jnp.concatenate([_tri_ones(c)] * 2, axis=1)
    i2 = _iota2((2 * c, w), 0)
    j2 = _iota2((2 * c, w), 1)
    same_head = (i2 // RWKV_HEAD) == (j2 // RWKV_HEAD)
    ir = _iota2((c, w), 0)
    jr = _iota2((c, w), 1) % c
    ir2 = _iota2((c, 2 * w), 0)
    jr2 = _iota2((c, 2 * w), 1) % c
    stack = functools.partial(_stack_heads, m0=m0)
    each = lambda f, *lists: [f(*xs) for xs in zip(*lists)]
    h_state = [state_ref[pi] for pi in range(n_pairs)]
    for c0 in range(0, nc, SCAN_GROUP_CHUNKS):
        where = [(pl.ds(ci * c, c), pl.ds(pi * w, w)) for ci in range(c0, min(c0 + SCAN_GROUP_CHUNKS, nc))
                 for pi in range(n_pairs)]
        load = lambda ref: [ref[rows, cols] for rows, cols in where]

        r, k, v, g, kk, bv = load(r_ref), load(k_ref), load(v_ref), load(g_ref), load(kk_ref), load(bv_ref)
        bc = each(lambda x: _cumsum_rows_fused(tri2, x), g)
        bl = each(lambda x: x[c - 1:c, :], bc)
        p_inv = each(lambda x: jnp.exp(-x), bc)
        p_end = each(lambda x, y: jnp.exp(x - y), bl, bc)
        rt = each(lambda a, x: a * jnp.exp(x), r, bc)
        at2 = each(lambda a, x, y: stack(-a * jnp.exp(x - y)).astype(BF16), kk, bc, g)
        v2 = each(lambda a: stack(a).astype(BF16), v)
        lhs = each(lambda a, x: jnp.concatenate([a, x.astype(BF16)], axis=0), at2, rt)
        rhs = each(lambda a, x, p: jnp.concatenate([stack(a * p), stack(x * p)], axis=0).astype(BF16), bv, k, p_inv)
        end = each(lambda a, x, p: jnp.concatenate([a * p, x * p], axis=0).astype(BF16), bv, k, p_end)

        prod = each(_mm_nt, lhs, rhs)
        low = each(lambda x: jnp.where(j2 < i2, x[0:2 * c, 0:w], 0.0), prod)
        a_ak = each(lambda x: jnp.where(jr < ir, x[0:c, w:2 * w] + x[c:2 * c, w:2 * w], 0.0).astype(BF16), prod)
        a_r = each(lambda x: jnp.where(jr2 <= ir2, x[2 * c:3 * c, :], 0.0).astype(BF16), prod)
        t_inv = _unit_lower_inverse(low, i2, j2)
        t_row = each(lambda t: (t[0:c, :] + t[c:2 * c, :]).astype(BF16), t_inv)
        z = each(_mm, a_ak, v2)
        w12 = each(lambda t, a, x: _mm(t, jnp.concatenate([a, stack(x).astype(BF16)], axis=1)), t_row, at2, z)
        w12b = each(lambda x: x.astype(BF16), w12)
        q12 = each(lambda a, x: _mm(a[:, 0:w], jnp.concatenate(
            [_stack_heads(x[:, 0:w], m0), _stack_heads(x[:, w:2 * w], m0)], axis=1)), a_r, w12b)
        q1 = each(lambda a, x: a + x[:, 0:w], rt, q12)
        q2 = each(lambda x, a, y: x[:, w:2 * w] + _mm(a[:, w:2 * w], y), q12, a_r, v2)
        zero_c = jnp.zeros((c, w), BF16)
        ti = each(lambda e, x, y: _mm_tn(e, jnp.concatenate(
            [x, jnp.concatenate([zero_c, y.astype(BF16)], axis=1)], axis=0)), end, w12b, v)
        trans = each(lambda x, e: jnp.where(same_head, x[:, 0:w], 0.0)
                     + jnp.where(i2 == j2, jnp.broadcast_to(jnp.exp(e), (2 * c, w)), 0.0), ti, bl)
        inject = each(lambda x: jnp.where(same_head, x[:, w:2 * w], 0.0), ti)
        q1t = each(lambda a, x: jnp.concatenate([a, x], axis=0).astype(BF16), q1, trans)

        for n, (rows, cols) in enumerate(where):
            pi = n % n_pairs
            yh = _mm(q1t[n], h_state[pi])
            y_ref[rows, cols] = yh[0:c, :] + q2[n]
            h_state[pi] = yh[c:3 * c, :] + inject[n]
    for pi in range(n_pairs):
        state_ref[pi] = h_state[pi]
    _rwkv_out_tile(h_ref, y_ref, r_ref, k_ref, v_ref, gate_ref, rk_ref, gg_ref, gb_ref, wout_ref, e_ref, et_ref, o_ref)


def _rwkv_scan(h, r, k, v, g, kk, bv, gate, r_k, gn_g, gn_b, w_out, e, et, *, tm):
    bsz, t, d = r.shape
    w = 2 * RWKV_HEAD
    row = pl.BlockSpec((None, tm, d), lambda b, p, i: (b, i, 0))
    vec = _full_spec((1, d))
    return pl.pallas_call(
        _rwkv_scan_body,
        grid=(bsz, 1, t // tm),
        in_specs=[row] * 8 + [vec, vec, vec, _full_spec(w_out.shape), _full_spec(e.shape), _full_spec(et.shape)],
        out_specs=row,
        out_shape=jax.ShapeDtypeStruct((bsz, t, d), F32),
        scratch_shapes=[pltpu.VMEM((d // w, w, w), F32), pltpu.VMEM((tm, d), F32)],
        compiler_params=_params(3),
        name="rwkv_scan",
    )(r, k, v, g, kk, bv, gate, h, r_k, gn_g, gn_b, w_out, e, et)


def _rwkv_out_tile(h_ref, y_ref, r_ref, k_ref, v_ref, gate_ref, rk_ref, gg_ref, gb_ref, wout_ref,
                   e_ref, et_ref, o_ref):
    tm = h_ref.shape[0]
    subs = [pl.ds(s0, SUB_ROWS) for s0 in range(0, tm, SUB_ROWS)]
    each = lambda f, *lists: [f(*xs) for xs in zip(*lists)]
    reduce_ = lambda z: _head_sums_reduce(z, e_ref[...])
    expand = lambda s: _head_sums_expand(s, et_ref[...])
    y = [y_ref[rows, :] for rows in subs]
    rk = [r_ref[rows, :] * k_ref[rows, :] * rk_ref[...] for rows in subs]
    y_sum, rk_sum = each(reduce_, y), each(reduce_, rk)
    y_sum, rk_sum = each(expand, y_sum), each(expand, rk_sum)
    dy = each(lambda a, s: a - s * (1.0 / RWKV_HEAD), y, y_sum)
    var = each(expand, each(lambda a: reduce_(a * a), dy))
    for rows, dy_, var_, rk_ in zip(subs, dy, var, rk_sum):
        yn = dy_ * lax.rsqrt(var_ * (1.0 / RWKV_HEAD) + RWKV_GN_EPS) * gg_ref[...] + gb_ref[...]
        out = (yn + rk_ * v_ref[rows, :]) * _silu(gate_ref[rows, :])
        o_ref[rows, :] = h_ref[rows, :] + _mm(out, wout_ref[...])


def _hgrn_body(h_ref, ng_ref, win_ref, lbl_ref, gg_ref, wout_ref, o_ref, state_ref, *, layer):
    tm, d = h_ref.shape
    c = CHUNK
    nh = d // HGRN_HEAD

    @pl.when(pl.program_id(1) == 0)
    def _():
        state_ref[...] = jnp.zeros_like(state_ref)

    logits = lbl_ref[...]
    ex = jnp.exp(logits - jnp.max(logits, axis=0, keepdims=True))
    lb = jnp.sum(ex[1:layer + 1, :], axis=0, keepdims=True) / jnp.sum(ex, axis=0, keepdims=True)

    nc = SUB_ROWS // c
    tri2 = jnp.concatenate([_tri_ones(c)] * 2, axis=1)
    row = _iota2((c, HGRN_HEAD), 0)
    top = row < HALF
    ri, ci_ = _iota2((c, c), 0), _iota2((c, c), 1)
    diag_blocks = (ri // HALF) == (ci_ // HALF)
    causal = ri >= ci_
    each = lambda f, *lists: [f(*xs) for xs in zip(*lists)]
    where = [(slice(ci * c, (ci + 1) * c), slice(hi * HGRN_HEAD, (hi + 1) * HGRN_HEAD))
             for ci in range(nc) for hi in range(nh)]
    take = lambda z: [z[rows, cols] for rows, cols in where]
    s_t = [state_ref[hi] for hi in range(nh)]

    project = lambda x: _mm(_rmsnorm(x, ng_ref[...]), win_ref[...])
    for sub_rows, x, p in _projected_ahead(h_ref, tm, project):
        q = p[:, 0:d]
        f_pre = p[:, d:2 * d]
        v = p[:, 2 * d:3 * d]
        gate = p[:, 3 * d:4 * d]
        sig = _sigmoid(f_pre)
        log_f = jnp.log(lb + (1.0 - lb) * sig)
        k = (1.0 - lb) * (1.0 - sig)
        qs, ks, vs, gs = take(q), take(k), take(v), take(log_f)
        b = each(lambda x: _cumsum_rows_fused(tri2, x), gs)
        bl = each(lambda x: x[c - 1:c], b)
        beta = each(lambda x: x[HALF - 1:HALF], b)
        m_a = each(lambda x: x[HALF // 2:HALF // 2 + 1], b)
        m_b = each(lambda x: x[HALF + HALF // 2:HALF + HALF // 2 + 1], b)
        mid = each(lambda x, y: jnp.where(top, x, y), m_a, m_b)
        q_df = each(lambda a, x, m: a * jnp.exp(x - m), qs, b, mid)
        k_df = each(lambda a, x, m: a * jnp.exp(m - x), ks, b, mid)
        q_d = each(lambda a: a.astype(BF16), q_df)
        k_d = each(lambda a: a.astype(BF16), k_df)
        q_o = each(lambda a, y, e: jnp.where(top, 0.0, a * jnp.exp(y - e)).astype(BF16), q_df, m_b, beta)
        k_o = each(lambda a, x, e: jnp.where(top, a * jnp.exp(e - x), 0.0).astype(BF16), k_df, m_a, beta)
        q_in = each(lambda a, x, y: (a * jnp.where(top, jnp.exp(x), jnp.exp(y))).astype(BF16),
                    q_df, m_a, m_b)
        k_out = each(lambda a, x, y, e: (a * jnp.where(top, jnp.exp(e - x), jnp.exp(e - y))).astype(BF16),
                     k_df, m_a, m_b, bl)
        v_b = each(lambda a: a.astype(BF16), vs)
        att = each(lambda qd, kd, qo, ko: jnp.where(diag_blocks, jnp.where(causal, _mm_nt(qd, kd), 0.0), _mm_nt(qo, ko)),
                   q_d, k_d, q_o, k_o)
        o_intra = each(_mm, att, v_b)
        kv = each(_mm_tn, v_b, k_out)
        s_in = []
        for n in range(len(where)):
            hi = n % nh
            s_in.append(s_t[hi])
            s_t[hi] = s_t[hi] * jnp.exp(bl[n]) + kv[n]
        o = each(lambda oi, a, s: oi + _mm_nt(a, s), o_intra, q_in, s_in)
        o = each(lambda z: z * lax.rsqrt(jnp.mean(z * z, axis=-1, keepdims=True) + RMS_EPS), o)
        o_all = jnp.concatenate([jnp.concatenate(o[ci * nh:(ci + 1) * nh], axis=1) for ci in range(nc)], axis=0)
        o_all = o_all * gg_ref[...] * _silu(gate)
        o_ref[sub_rows, :] = x + _mm(o_all, wout_ref[...])
    for hi in range(nh):
        state_ref[hi] = s_t[hi]


def _hgrn_layer(h, ng, w_in, lb_logits, gn_g, w_out, *, layer, tm):
    bsz, t, d = h.shape
    row = _row_spec(tm, d)
    vec = _full_spec((1, d))
    return pl.pallas_call(
        functools.partial(_hgrn_body, layer=layer),
        grid=(bsz, t // tm),
        in_specs=[row, vec, _full_spec(w_in.shape), _full_spec(lb_logits.shape), vec, _full_spec(w_out.shape)],
        out_specs=row,
        out_shape=jax.ShapeDtypeStruct((bsz, t, d), F32),
        scratch_shapes=[pltpu.VMEM((d // HGRN_HEAD, HGRN_HEAD, HGRN_HEAD), F32)],
        compiler_params=_params(2),
        name="hgrn_layer",
    )(h, ng, w_in, lb_logits, gn_g, w_out)


def _conv_body(h_ref, ng_ref, win_ref, cw_ref, wout_ref, o_ref, carry_ref):
    tm, d = h_ref.shape

    @pl.when(pl.program_id(1) == 0)
    def _():
        carry_ref[...] = jnp.zeros_like(carry_ref)

    project = lambda x: _mm(_rmsnorm(x, ng_ref[...]), win_ref[...])
    tail = carry_ref[...]
    for rows, x, p in _projected_ahead(h_ref, tm, project):
        b_gate = p[:, 0:d]
        y = p[:, d:2 * d] * p[:, 2 * d:3 * d]
        gate = p[:, 3 * d:4 * d]
        y1 = _shift_rows(y, tail, 1)
        y2 = _shift_rows(y, tail, 2)
        tail = y[SUB_ROWS - SUBLANES:SUB_ROWS, :]
        yc = cw_ref[0:1, :] * y2 + cw_ref[1:2, :] * y1 + cw_ref[2:3, :] * y
        o_ref[rows, :] = x + _mm(b_gate * yc * _silu(gate), wout_ref[...])
    carry_ref[...] = tail


def _conv_layer(h, ng, w_in, conv_w, w_out, *, tm):
    bsz, t, d = h.shape
    row = _row_spec(tm, d)
    return pl.pallas_call(
        _conv_body,
        grid=(bsz, t // tm),
        in_specs=[row, _full_spec((1, d)), _full_spec(w_in.shape), _full_spec(conv_w.shape), _full_spec(w_out.shape)],
        out_specs=row,
        out_shape=jax.ShapeDtypeStruct((bsz, t, d), F32),
        scratch_shapes=[pltpu.VMEM((SUBLANES, d), F32)],
        compiler_params=_params(2),
        name="conv_layer",
    )(h, ng, w_in, conv_w, w_out)


def _gmlp_body(h_ref, ng_ref, win_ref, vg_ref, ws_ref, bs_ref, wout_ref, fg_ref, o_ref):
    tm, d = h_ref.shape
    c = GMLP_CHUNK
    causal = _iota2((c, c), 0) >= _iota2((c, c), 1)
    bias = bs_ref[...]
    nc = SUB_ROWS // c
    project = lambda x: _mm(_rmsnorm(x, ng_ref[...]), win_ref[...])
    for rows, x, p in _projected_ahead(h_ref, tm, project):
        u = p[:, 0:d]
        vb = _rmsnorm(p[:, d:2 * d], vg_ref[...]).astype(BF16)
        gate = p[:, 2 * d:3 * d]
        parts = []
        for gi in range(d // GMLP_GROUP):
            cols = slice(gi * GMLP_GROUP, (gi + 1) * GMLP_GROUP)
            v_g = jnp.concatenate([vb[ci * c:(ci + 1) * c, cols] for ci in range(nc)], axis=1)
            parts.append(_mm(jnp.where(causal, ws_ref[gi], 0.0), v_g))
        s = jnp.concatenate(
            [jnp.concatenate([q[:, ci * GMLP_GROUP:(ci + 1) * GMLP_GROUP] for q in parts], axis=1) + bias
             for ci in range(nc)], axis=0)
        out = x + _mm(u * s * _silu(gate), wout_ref[...])
        o_ref[rows, :] = _rmsnorm(out, fg_ref[...])


def _gmlp_layer(h, ng, w_in, v_g, w_s, bs_full, w_out, final_g, *, tm):
    bsz, t, d = h.shape
    row = _row_spec(tm, d)
    vec = _full_spec((1, d))
    return pl.pallas_call(
        _gmlp_body,
        grid=(bsz, t // tm),
        in_specs=[row, vec, _full_spec(w_in.shape), vec, _full_spec(w_s.shape), _full_spec(bs_full.shape),
                  _full_spec(w_out.shape), vec],
        out_specs=row,
        out_shape=jax.ShapeDtypeStruct((bsz, t, d), F32),
        compiler_params=_params(2),
        name="gmlp_layer",
    )(h, ng, w_in, v_g, w_s, bs_full, w_out, final_g)


def _rwkv_layer(h, ng, w_in, mu, w0, w_w2, a0, w_a2, k_k, k_a, r_k, gn_g, gn_b, w_out):
    d = h.shape[-1]
    zeros = jnp.zeros((RWKV_LORA, d), F32)
    ww_pad = jnp.concatenate([w_w2, zeros], axis=0).astype(BF16)
    wa_pad = jnp.concatenate([zeros, w_a2], axis=0).astype(BF16)
    e = (jnp.arange(d)[:, None] // RWKV_HEAD == jnp.arange(LANES)[None, :]).astype(BF16)
    et = jnp.concatenate([e.T, e.T], axis=0)
    vec = lambda z: z.reshape(1, -1)
    r, k, v, g, kk, bv, gate = _rwkv_prep(h, vec(ng), w_in.astype(BF16), vec(mu), vec(w0), ww_pad, vec(a0), wa_pad,
                                          vec(k_k), vec(k_a), e, et, tm=_tile_rows("rwkv_prep"))
    return _rwkv_scan(h, r, k, v, g, kk, bv, gate, vec(r_k), vec(gn_g), vec(gn_b), w_out.astype(BF16), e, et,
                      tm=_tile_rows("rwkv_scan"))


def kernel(x, norm_g, final_g, rwkv_w_in, rwkv_mu, rwkv_w0, rwkv_w_w2, rwkv_a0, rwkv_w_a2, rwkv_k_k, rwkv_k_a, rwkv_r_k, rwkv_gn_g, rwkv_gn_b, rwkv_w_out, hgrn_lb_logits, hgrn_w_in, hgrn_gn_g, hgrn_w_out, conv_w_in, conv_w, conv_w_out, gmlp_w_in, gmlp_v_g, gmlp_w_s, gmlp_b_s, gmlp_w_out):
    depth = norm_g.shape[0]
    assert depth == 4, "the fused final norm assumes the gMLP layer is the last one"
    vec = lambda z: z.reshape(1, -1)
    h = x
    h = _rwkv_layer(h, norm_g[0], rwkv_w_in[0], rwkv_mu[0], rwkv_w0[0], rwkv_w_w2[0], rwkv_a0[0], rwkv_w_a2[0],
                    rwkv_k_k[0], rwkv_k_a[0], rwkv_r_k[0], rwkv_gn_g[0], rwkv_gn_b[0], rwkv_w_out[0])
    h = _hgrn_layer(h, vec(norm_g[1]), hgrn_w_in[0].astype(BF16), hgrn_lb_logits, vec(hgrn_gn_g[0]),
                    hgrn_w_out[0].astype(BF16), layer=1, tm=_tile_rows("hgrn"))
    h = _conv_layer(h, vec(norm_g[2]), conv_w_in[0].astype(BF16), conv_w[0], conv_w_out[0].astype(BF16),
                    tm=_tile_rows("conv"))
    bs_full = jnp.repeat(gmlp_b_s[0].T, GMLP_GROUP, axis=1)
    h = _gmlp_layer(h, vec(norm_g[3]), gmlp_w_in[0].astype(BF16), vec(gmlp_v_g[0]), gmlp_w_s[0].astype(BF16),
                    bs_full, gmlp_w_out[0].astype(BF16), vec(final_g), tm=_tile_rows("gmlp"))
    return h
```

```python
import functools

import jax
import jax.numpy as jnp
from jax import lax
from jax.experimental import pallas as pl
from jax.experimental.pallas import tpu as pltpu

F32 = jnp.float32
BF16 = jnp.bfloat16

RMS_EPS = 1e-6
RWKV_HEAD = 64
RWKV_GN_EPS = 64e-5
RWKV_LORA = 64
DECAY_SCALE = 0.6065306597126334
HGRN_HEAD = 128
GMLP_CHUNK = 128
GMLP_GROUP = 128
CONV_WIDTH = 3

LANES = 128
SUBLANES = 8
V7X_VMEM_BYTES = 64 * 1024 * 1024
VMEM_LIMIT = V7X_VMEM_BYTES * 7 // 8

CHUNK = 64
HALF = CHUNK // 2
SUB_ROWS = 256
SCAN_GROUP_CHUNKS = 2


def _tile_rows(name):
    return {"rwkv_prep": 2 * SUB_ROWS, "rwkv_scan": 2 * SUB_ROWS, "hgrn": 4 * SUB_ROWS, "conv": 4 * SUB_ROWS,
            "gmlp": 4 * SUB_ROWS}[name]


def _mm(a, b):
    return jnp.dot(a.astype(BF16), b.astype(BF16), preferred_element_type=F32)


def _mm_nt(a, b):
    return lax.dot_general(a.astype(BF16), b.astype(BF16), (((1,), (1,)), ((), ())),
                           preferred_element_type=F32)


def _mm_tn(a, b):
    return lax.dot_general(a.astype(BF16), b.astype(BF16), (((0,), (0,)), ((), ())),
                           preferred_element_type=F32)


def _split2(x):
    hi = x.astype(BF16)
    lo = (x - hi.astype(F32)).astype(BF16)
    return hi, lo


def _head_sums(x, e, et2):
    return _head_sums_expand(_head_sums_reduce(x, e), et2)


def _head_sums_reduce(x, e):
    return jnp.dot(x.astype(BF16), e, preferred_element_type=F32)


def _head_sums_expand(s, et2):
    hi, lo = _split2(s)
    return jnp.dot(jnp.concatenate([hi, lo], axis=1), et2, preferred_element_type=F32)


def _cumsum_rows_fused(tri2, g):
    return jnp.dot(tri2, jnp.concatenate(_split2(g), axis=0), preferred_element_type=F32)


def _sigmoid(x):
    return 0.5 * jnp.tanh(0.5 * x) + 0.5


def _silu(x):
    return x * _sigmoid(x)


def _log1pexp_neg(z):
    return jnp.log(1.0 + jnp.exp(-z))


def _softplus(x):
    return jnp.maximum(x, 0.0) + _log1pexp_neg(jnp.abs(x))


def _rmsnorm(x, g):
    ms = jnp.mean(x * x, axis=-1, keepdims=True)
    return x * lax.rsqrt(ms + RMS_EPS) * g


def _iota2(shape, axis):
    return lax.broadcasted_iota(jnp.int32, shape, axis)


def _shift_rows(x, tail, n):
    rolled = pltpu.roll(x, n, 0)
    head = rolled[0:SUBLANES]
    row = _iota2(head.shape, 0)
    head = jnp.where(row < n, pltpu.roll(tail, n, 0), head)
    return jnp.concatenate([head, rolled[SUBLANES:]], axis=0)


def _tri_ones(n):
    return (_iota2((n, n), 0) >= _iota2((n, n), 1)).astype(BF16)


def _projected_ahead(h_ref, tm, project):
    subs = [pl.ds(s0, SUB_ROWS) for s0 in range(0, tm, SUB_ROWS)]
    x = h_ref[subs[0], :]
    ahead = (x, project(x))
    for i, rows in enumerate(subs):
        cur = ahead
        if i + 1 < len(subs):
            x = h_ref[subs[i + 1], :]
            ahead = (x, project(x))
        yield rows, cur[0], cur[1]


def _row_spec(tm, d):
    return pl.BlockSpec((None, tm, d), lambda b, t: (b, t, 0))


def _full_spec(shape):
    n = len(shape)
    return pl.BlockSpec(shape, lambda *_: (0,) * n)


def _params(n_axes):
    return pltpu.CompilerParams(dimension_semantics=("arbitrary",) * n_axes, vmem_limit_bytes=VMEM_LIMIT)


def _rwkv_prep_body(h_ref, ng_ref, win_ref, mu_ref, w0_ref, ww_ref, a0_ref, wa_ref, kk_ref, ka_ref,
                    e_ref, et_ref,
                    r_out, k_out, v_out, g_out, kk_out, bv_out, gate_out, carry_ref, *, d):
    tm = h_ref.shape[0]

    @pl.when(pl.program_id(1) == 0)
    def _():
        carry_ref[...] = jnp.zeros_like(carry_ref)

    for s0 in range(0, tm, SUB_ROWS):
        rows = pl.ds(s0, SUB_ROWS)
        hn = _rmsnorm(h_ref[rows, :], ng_ref[...]).astype(BF16)

        def project(lo, hi):
            p = jnp.dot(hn, win_ref[:, lo:hi], preferred_element_type=F32)
            prev = _shift_rows(p, carry_ref[:, lo:hi], 1)
            carry_ref[:, lo:hi] = p[SUB_ROWS - SUBLANES:SUB_ROWS, :]
            return p + (prev - p) * mu_ref[:, lo:hi]

        lw = project(3 * d, 3 * d + 2 * RWKV_LORA)
        k = project(d, 2 * d)
        lw = jnp.where(_iota2(lw.shape, 1) < RWKV_LORA, jnp.tanh(lw), lw)
        dw = _mm(lw, ww_ref[...])
        da = _mm(lw, wa_ref[...])
        r_out[rows, :] = project(0, d)
        kk = k * kk_ref[...]
        n2 = _head_sums_reduce(kk * kk, e_ref[...])
        v_out[rows, :] = project(2 * d, 3 * d)
        n2 = _head_sums_expand(n2, et_ref[...])
        gate_out[rows, :] = project(3 * d + 2 * RWKV_LORA, win_ref.shape[1])
        a = _sigmoid(a0_ref[...] + da)
        kk = kk * lax.rsqrt(jnp.maximum(n2, 1e-24))
        k_out[rows, :] = k * (1.0 + (a - 1.0) * ka_ref[...])
        g_out[rows, :] = -DECAY_SCALE * _sigmoid(w0_ref[...] + dw)
        kk_out[rows, :] = kk
        bv_out[rows, :] = kk * a


def _rwkv_prep(h, ng, w_in, mu, w0, ww_pad, a0, wa_pad, k_k, k_a, e, et, *, tm):
    bsz, t, d = h.shape
    cols = w_in.shape[1]
    row = _row_spec(tm, d)
    vec = _full_spec((1, d))
    out = jax.ShapeDtypeStruct((bsz, t, d), F32)
    return pl.pallas_call(
        functools.partial(_rwkv_prep_body, d=d),
        grid=(bsz, t // tm),
        in_specs=[row, vec, _full_spec(w_in.shape), _full_spec((1, cols)), vec, _full_spec(ww_pad.shape), vec,
                  _full_spec(wa_pad.shape), vec, vec, _full_spec(e.shape), _full_spec(et.shape)],
        out_specs=[row] * 7,
        out_shape=[out] * 7,
        scratch_shapes=[pltpu.VMEM((SUBLANES, cols), F32)],
        compiler_params=_params(2),
        name="rwkv_prep",
    )(h, ng, w_in, mu, w0, ww_pad, a0, wa_pad, k_k, k_a, e, et)


def _stack_heads(x, m0):
    return jnp.concatenate([jnp.where(m0, x, 0.0), jnp.where(m0, 0.0, x)], axis=0)


def _unit_lower_inverse(lows, i2, j2):
    s = 1
    eye = jnp.where(i2 == j2, 1.0, 0.0)
    t_inv = [eye for _ in lows]
    while s < CHUNK:
        keep = ((i2 // (2 * s)) == (j2 // (2 * s))) & ((i2 // s) != (j2 // s))
        subs = [jnp.where(keep, low, 0.0).astype(BF16) for low in lows]
        if s == 1:
            t_inv = [t + sub for t, sub in zip(t_inv, subs)]
        else:
            t_bf = [t.astype(BF16) for t in t_inv]
            left = [_mm(tb, sub) for tb, sub in zip(t_bf, subs)]
            t_inv = [t + _mm(lt, tb) for t, lt, tb in zip(t_inv, left, t_bf)]
        s *= 2
    return t_inv


def _rwkv_scan_body(r_ref, k_ref, v_ref, g_ref, kk_ref, bv_ref, gate_ref, h_ref, rk_ref, gg_ref, gb_ref,
                    wout_ref, e_ref, et_ref, o_ref, state_ref, y_ref):
    tm = r_ref.shape[0]
    c = CHUNK
    w = 2 * RWKV_HEAD
    nc = tm // c
    n_pairs = r_ref.shape[1] // w

    @pl.when(pl.program_id(2) == 0)
    def _():
        state_ref[...] = jnp.zeros_like(state_ref)

    m0 = _iota2((c, w), 1) < RWKV_HEAD
    tri2 = jnp.concatenate([_tri_ones(c)] * 2, axis=1)
    i2 = _iota2((2 * c, w), 0)
    j2 = _iota2((2 * c, w), 1)
    same_head = (i2 // RWKV_HEAD) == (j2 // RWKV_HEAD)
    ir = _iota2((c, w), 0)
    jr = _iota2((c, w), 1) % c
    ir2 = _iota2((c, 2 * w), 0)
    jr2 = _iota2((c, 2 * w), 1) % c
    stack = functools.partial(_stack_heads, m0=m0)
    each = lambda f, *lists: [f(*xs) for xs in zip(*lists)]
    h_state = [state_ref[pi] for pi in range(n_pairs)]
    for c0 in range(0, nc, SCAN_GROUP_CHUNKS):
        where = [(pl.ds(ci * c, c), pl.ds(pi * w, w)) for ci in range(c0, min(c0 + SCAN_GROUP_CHUNKS, nc))
                 for pi in range(n_pairs)]
        load = lambda ref: [ref[rows, cols] for rows, cols in where]

        r, k, v, g, kk, bv = load(r_ref), load(k_ref), load(v_ref), load(g_ref), load(kk_ref), load(bv_ref)
        bc = each(lambda x: _cumsum_rows_fused(tri2, x), g)
        bl = each(lambda x: x[c - 1:c, :], bc)
        p_inv = each(lambda x: jnp.exp(-x), bc)
        p_end = each(lambda x, y: jnp.exp(x - y), bl, bc)
        rt = each(lambda a, x: a * jnp.exp(x), r, bc)
        at2 = each(lambda a, x, y: stack(-a * jnp.exp(x - y)).astype(BF16), kk, bc, g)
        v2 = each(lambda a: stack(a).astype(BF16), v)
        lhs = each(lambda a, x: jnp.concatenate([a, x.astype(BF16)], axis=0), at2, rt)
        rhs = each(lambda a, x, p: jnp.concatenate([stack(a * p), stack(x * p)], axis=0).astype(BF16), bv, k, p_inv)
        end = each(lambda a, x, p: jnp.concatenate([a * p, x * p], axis=0).astype(BF16), bv, k, p_end)

        prod = each(_mm_nt, lhs, rhs)
        low = each(lambda x: jnp.where(j2 < i2, x[0:2 * c, 0:w], 0.0), prod)
        a_ak = each(lambda x: jnp.where(jr < ir, x[0:c, w:2 * w] + x[c:2 * c, w:2 * w], 0.0).astype(BF16), prod)
        a_r = each(lambda x: jnp.where(jr2 <= ir2, x[2 * c:3 * c, :], 0.0).astype(BF16), prod)
        t_inv = _unit_lower_inverse(low, i2, j2)
        t_row = each(lambda t: (t[0:c, :] + t[c:2 * c, :]).astype(BF16), t_inv)
        z = each(_mm, a_ak, v2)
        w12 = each(lambda t, a, x: _mm(t, jnp.concatenate([a, stack(x).astype(BF16)], axis=1)), t_row, at2, z)
        w12b = each(lambda x: x.astype(BF16), w12)
        q12 = each(lambda a, x: _mm(a[:, 0:w], jnp.concatenate(
            [_stack_heads(x[:, 0:w], m0), _stack_heads(x[:, w:2 * w], m0)], axis=1)), a_r, w12b)
        q1 = each(lambda a, x: a + x[:, 0:w], rt, q12)
        q2 = each(lambda x, a, y: x[:, w:2 * w] + _mm(a[:, w:2 * w], y), q12, a_r, v2)
        zero_c = jnp.zeros((c, w), BF16)
        ti = each(lambda e, x, y: _mm_tn(e, jnp.concatenate(
            [x, jnp.concatenate([zero_c, y.astype(BF16)], axis=1)], axis=0)), end, w12b, v)
        trans = each(lambda x, e: jnp.where(same_head, x[:, 0:w], 0.0)
                     + jnp.where(i2 == j2, jnp.broadcast_to(jnp.exp(e), (2 * c, w)), 0.0), ti, bl)
        inject = each(lambda x: jnp.where(same_head, x[:, w:2 * w], 0.0), ti)
        q1t = each(lambda a, x: jnp.concatenate([a, x], axis=0).astype(BF16), q1, trans)

        for n, (rows, cols) in enumerate(where):
            pi = n % n_pairs
            yh = _mm(q1t[n], h_state[pi])
            y_ref[rows, cols] = yh[0:c, :] + q2[n]
            h_state[pi] = yh[c:3 * c, :] + inject[n]
    for pi in range(n_pairs):
        state_ref[pi] = h_state[pi]
    _rwkv_out_tile(h_ref, y_ref, r_ref, k_ref, v_ref, gate_ref, rk_ref, gg_ref, gb_ref, wout_ref, e_ref, et_ref, o_ref)


def _rwkv_scan(h, r, k, v, g, kk, bv, gate, r_k, gn_g, gn_b, w_out, e, et, *, tm):
    bsz, t, d = r.shape
    w = 2 * RWKV_HEAD
    row = pl.BlockSpec((None, tm, d), lambda b, p, i: (b, i, 0))
    vec = _full_spec((1, d))
    return pl.pallas_call(
        _rwkv_scan_body,
        grid=(bsz, 1, t // tm),
        in_specs=[row] * 8 + [vec, vec, vec, _full_spec(w_out.shape), _full_spec(e.shape), _full_spec(et.shape)],
        out_specs=row,
        out_shape=jax.ShapeDtypeStruct((bsz, t, d), F32),
        scratch_shapes=[pltpu.VMEM((d // w, w, w), F32), pltpu.VMEM((tm, d), F32)],
        compiler_params=_params(3),
        name="rwkv_scan",
    )(r, k, v, g, kk, bv, gate, h, r_k, gn_g, gn_b, w_out, e, et)


def _rwkv_out_tile(h_ref, y_ref, r_ref, k_ref, v_ref, gate_ref, rk_ref, gg_ref, gb_ref, wout_ref,
                   e_ref, et_ref, o_ref):
    tm = h_ref.shape[0]
    subs = [pl.ds(s0, SUB_ROWS) for s0 in range(0, tm, SUB_ROWS)]
    each = lambda f, *lists: [f(*xs) for xs in zip(*lists)]
    reduce_ = lambda z: _head_sums_reduce(z, e_ref[...])
    expand = lambda s: _head_sums_expand(s, et_ref[...])
    y = [y_ref[rows, :] for rows in subs]
    rk = [r_ref[rows, :] * k_ref[rows, :] * rk_ref[...] for rows in subs]
    y_sum, rk_sum = each(reduce_, y), each(reduce_, rk)
    y_sum, rk_sum = each(expand, y_sum), each(expand, rk_sum)
    dy = each(lambda a, s: a - s * (1.0 / RWKV_HEAD), y, y_sum)
    var = each(expand, each(lambda a: reduce_(a * a), dy))
    for rows, dy_, var_, rk_ in zip(subs, dy, var, rk_sum):
        yn = dy_ * lax.rsqrt(var_ * (1.0 / RWKV_HEAD) + RWKV_GN_EPS) * gg_ref[...] + gb_ref[...]
        out = (yn + rk_ * v_ref[rows, :]) * _silu(gate_ref[rows, :])
        o_ref[rows, :] = h_ref[rows, :] + _mm(out, wout_ref[...])


def _hgrn_body(h_ref, ng_ref, win_ref, lbl_ref, gg_ref, wout_ref, o_ref, state_ref, *, layer):
    tm, d = h_ref.shape
    c = CHUNK
    nh = d // HGRN_HEAD

    @pl.when(pl.program_id(1) == 0)
    def _():
        state_ref[...] = jnp.zeros_like(state_ref)

    logits = lbl_ref[...]
    ex = jnp.exp(logits - jnp.max(logits, axis=0, keepdims=True))
    lb = jnp.sum(ex[1:layer + 1, :], axis=0, keepdims=True) / jnp.sum(ex, axis=0, keepdims=True)

    nc = SUB_ROWS // c
    tri2 = jnp.concatenate([_tri_ones(c)] * 2, axis=1)
    row = _iota2((c, HGRN_HEAD), 0)
    top = row < HALF
    ri, ci_ = _iota2((c, c), 0), _iota2((c, c), 1)
    diag_blocks = (ri // HALF) == (ci_ // HALF)
    causal = ri >= ci_
    each = lambda f, *lists: [f(*xs) for xs in zip(*lists)]
    where = [(slice(ci * c, (ci + 1) * c), slice(hi * HGRN_HEAD, (hi + 1) * HGRN_HEAD))
             for ci in range(nc) for hi in range(nh)]
    take = lambda z: [z[rows, cols] for rows, cols in where]
    s_t = [state_ref[hi] for hi in range(nh)]

    project = lambda x: _mm(_rmsnorm(x, ng_ref[...]), win_ref[...])
    for sub_rows, x, p in _projected_ahead(h_ref, tm, project):
        q = p[:, 0:d]
        f_pre = p[:, d:2 * d]
        v = p[:, 2 * d:3 * d]
        gate = p[:, 3 * d:4 * d]
        sig = _sigmoid(f_pre)
        log_f = jnp.log(lb + (1.0 - lb) * sig)
        k = (1.0 - lb) * (1.0 - sig)
        qs, ks, vs, gs = take(q), take(k), take(v), take(log_f)
        b = each(lambda x: _cumsum_rows_fused(tri2, x), gs)
        bl = each(lambda x: x[c - 1:c], b)
        beta = each(lambda x: x[HALF - 1:HALF], b)
        m_a = each(lambda x: x[HALF // 2:HALF // 2 + 1], b)
        m_b = each(lambda x: x[HALF + HALF // 2:HALF + HALF // 2 + 1], b)
        mid = each(lambda x, y: jnp.where(top, x, y), m_a, m_b)
        q_df = each(lambda a, x, m: a * jnp.exp(x - m), qs, b, mid)
        k_df = each(lambda a, x, m: a * jnp.exp(m - x), ks, b, mid)
        q_d = each(lambda a: a.astype(BF16), q_df)
        k_d = each(lambda a: a.astype(BF16), k_df)
        q_o = each(lambda a, y, e: jnp.where(top, 0.0, a * jnp.exp(y - e)).astype(BF16), q_df, m_b, beta)
        k_o = each(lambda a, x, e: jnp.where(top, a * jnp.exp(e - x), 0.0).astype(BF16), k_df, m_a, beta)
        q_in = each(lambda a, x, y: (a * jnp.where(top, jnp.exp(x), jnp.exp(y))).astype(BF16),
                    q_df, m_a, m_b)
        k_out = each(lambda a, x, y, e: (a * jnp.where(top, jnp.exp(e - x), jnp.exp(e - y))).astype(BF16),
                     k_df, m_a, m_b, bl)
        v_b = each(lambda a: a.astype(BF16), vs)
        att = each(lambda qd, kd, qo, ko: jnp.where(diag_blocks, jnp.where(causal, _mm_nt(qd, kd), 0.0), _mm_nt(qo, ko)),
                   q_d, k_d, q_o, k_o)
        o_intra = each(_mm, att, v_b)
        kv = each(_mm_tn, v_b, k_out)
        s_in = []
        for n in range(len(where)):
            hi = n % nh
            s_in.append(s_t[hi])
            s_t[hi] = s_t[hi] * jnp.exp(bl[n]) + kv[n]
        o = each(lambda oi, a, s: oi + _mm_nt(a, s), o_intra, q_in, s_in)
        o = each(lambda z: z * lax.rsqrt(jnp.mean(z * z, axis=-1, keepdims=True) + RMS_EPS), o)
        o_all = jnp.concatenate([jnp.concatenate(o[ci * nh:(ci + 1) * nh], axis=1) for ci in range(nc)], axis=0)
        o_all = o_all * gg_ref[...] * _silu(gate)
        o_ref[sub_rows, :] = x + _mm(o_all, wout_ref[...])
    for hi in range(nh):
        state_ref[hi] = s_t[hi]


def _hgrn_layer(h, ng, w_in, lb_logits, gn_g, w_out, *, layer, tm):
    bsz, t, d = h.shape
    row = _row_spec(tm, d)
    vec = _full_spec((1, d))
    return pl.pallas_call(
        functools.partial(_hgrn_body, layer=layer),
        grid=(bsz, t // tm),
        in_specs=[row, vec, _full_spec(w_in.shape), _full_spec(lb_logits.shape), vec, _full_spec(w_out.shape)],
        out_specs=row,
        out_shape=jax.ShapeDtypeStruct((bsz, t, d), F32),
        scratch_shapes=[pltpu.VMEM((d // HGRN_HEAD, HGRN_HEAD, HGRN_HEAD), F32)],
        compiler_params=_params(2),
        name="hgrn_layer",
    )(h, ng, w_in, lb_logits, gn_g, w_out)


def _conv_body(h_ref, ng_ref, win_ref, cw_ref, wout_ref, o_ref, carry_ref):
    tm, d = h_ref.shape

    @pl.when(pl.program_id(1) == 0)
    def _():
        carry_ref[...] = jnp.zeros_like(carry_ref)

    project = lambda x: _mm(_rmsnorm(x, ng_ref[...]), win_ref[...])
    tail = carry_ref[...]
    for rows, x, p in _projected_ahead(h_ref, tm, project):
        b_gate = p[:, 0:d]
        y = p[:, d:2 * d] * p[:, 2 * d:3 * d]
        gate = p[:, 3 * d:4 * d]
        y1 = _shift_rows(y, tail, 1)
        y2 = _shift_rows(y, tail, 2)
        tail = y[SUB_ROWS - SUBLANES:SUB_ROWS, :]
        yc = cw_ref[0:1, :] * y2 + cw_ref[1:2, :] * y1 + cw_ref[2:3, :] * y
        o_ref[rows, :] = x + _mm(b_gate * yc * _silu(gate), wout_ref[...])
    carry_ref[...] = tail


def _conv_layer(h, ng, w_in, conv_w, w_out, *, tm):
    bsz, t, d = h.shape
    row = _row_spec(tm, d)
    return pl.pallas_call(
        _conv_body,
        grid=(bsz, t // tm),
        in_specs=[row, _full_spec((1, d)), _full_spec(w_in.shape), _full_spec(conv_w.shape), _full_spec(w_out.shape)],
        out_specs=row,
        out_shape=jax.ShapeDtypeStruct((bsz, t, d), F32),
        scratch_shapes=[pltpu.VMEM((SUBLANES, d), F32)],
        compiler_params=_params(2),
        name="conv_layer",
    )(h, ng, w_in, conv_w, w_out)


def _gmlp_body(h_ref, ng_ref, win_ref, vg_ref, ws_ref, bs_ref, wout_ref, fg_ref, o_ref):
    tm, d = h_ref.shape
    c = GMLP_CHUNK
    causal = _iota2((c, c), 0) >= _iota2((c, c), 1)
    bias = bs_ref[...]
    nc = SUB_ROWS // c
    project = lambda x: _mm(_rmsnorm(x, ng_ref[...]), win_ref[...])
    for rows, x, p in _projected_ahead(h_ref, tm, project):
        u = p[:, 0:d]
        vb = _rmsnorm(p[:, d:2 * d], vg_ref[...]).astype(BF16)
        gate = p[:, 2 * d:3 * d]
        parts = []
        for gi in range(d // GMLP_GROUP):
            cols = slice(gi * GMLP_GROUP, (gi + 1) * GMLP_GROUP)
            v_g = jnp.concatenate([vb[ci * c:(ci + 1) * c, cols] for ci in range(nc)], axis=1)
            parts.append(_mm(jnp.where(causal, ws_ref[gi], 0.0), v_g))
        s = jnp.concatenate(
            [jnp.concatenate([q[:, ci * GMLP_GROUP:(ci + 1) * GMLP_GROUP] for q in parts], axis=1) + bias
             for ci in range(nc)], axis=0)
        out = x + _mm(u * s * _silu(gate), wout_ref[...])
        o_ref[rows, :] = _rmsnorm(out, fg_ref[...])


def _gmlp_layer(h, ng, w_in, v_g, w_s, bs_full, w_out, final_g, *, tm):
    bsz, t, d = h.shape
    row = _row_spec(tm, d)
    vec = _full_spec((1, d))
    return pl.pallas_call(
        _gmlp_body,
        grid=(bsz, t // tm),
        in_specs=[row, vec, _full_spec(w_in.shape), vec, _full_spec(w_s.shape), _full_spec(bs_full.shape),
                  _full_spec(w_out.shape), vec],
        out_specs=row,
        out_shape=jax.ShapeDtypeStruct((bsz, t, d), F32),
        compiler_params=_params(2),
        name="gmlp_layer",
    )(h, ng, w_in, v_g, w_s, bs_full, w_out, final_g)


def _rwkv_layer(h, ng, w_in, mu, w0, w_w2, a0, w_a2, k_k, k_a, r_k, gn_g, gn_b, w_out):
    d = h.shape[-1]
    zeros = jnp.zeros((RWKV_LORA, d), F32)
    ww_pad = jnp.concatenate([w_w2, zeros], axis=0).astype(BF16)
    wa_pad = jnp.concatenate([zeros, w_a2], axis=0).astype(BF16)
    e = (jnp.arange(d)[:, None] // RWKV_HEAD == jnp.arange(LANES)[None, :]).astype(BF16)
    et = jnp.concatenate([e.T, e.T], axis=0)
    vec = lambda z: z.reshape(1, -1)
    r, k, v, g, kk, bv, gate = _rwkv_prep(h, vec(ng), w_in.astype(BF16), vec(mu), vec(w0), ww_pad, vec(a0), wa_pad,
                                          vec(k_k), vec(k_a), e, et, tm=_tile_rows("rwkv_prep"))
    return _rwkv_scan(h, r, k, v, g, kk, bv, gate, vec(r_k), vec(gn_g), vec(gn_b), w_out.astype(BF16), e, et,
                      tm=_tile_rows("rwkv_scan"))


def kernel(x, norm_g, final_g, rwkv_w_in, rwkv_mu, rwkv_w0, rwkv_w_w2, rwkv_a0, rwkv_w_a2, rwkv_k_k, rwkv_k_a, rwkv_r_k, rwkv_gn_g, rwkv_gn_b, rwkv_w_out, hgrn_lb_logits, hgrn_w_in, hgrn_gn_g, hgrn_w_out, conv_w_in, conv_w, conv_w_out, gmlp_w_in, gmlp_v_g, gmlp_w_s, gmlp_b_s, gmlp_w_out):
    depth = norm_g.shape[0]
    assert depth == 4, "the fused final norm assumes the gMLP layer is the last one"
    vec = lambda z: z.reshape(1, -1)
    h = x
    h = _rwkv_layer(h, norm_g[0], rwkv_w_in[0], rwkv_mu[0], rwkv_w0[0], rwkv_w_w2[0], rwkv_a0[0], rwkv_w_a2[0],
                    rwkv_k_k[0], rwkv_k_a[0], rwkv_r_k[0], rwkv_gn_g[0], rwkv_gn_b[0], rwkv_w_out[0])
    h = _hgrn_layer(h, vec(norm_g[1]), hgrn_w_in[0].astype(BF16), hgrn_lb_logits, vec(hgrn_gn_g[0]),
                    hgrn_w_out[0].astype(BF16), layer=1, tm=_tile_rows("hgrn"))
    h = _conv_layer(h, vec(norm_g[2]), conv_w_in[0].astype(BF16), conv_w[0], conv_w_out[0].astype(BF16),
                    tm=_tile_rows("conv"))
    bs_full = jnp.repeat(gmlp_b_s[0].T, GMLP_GROUP, axis=1)
    h = _gmlp_layer(h, vec(norm_g[3]), gmlp_w_in[0].astype(BF16), vec(gmlp_v_g[0]), gmlp_w_s[0].astype(BF16),
                    bs_full, gmlp_w_out[0].astype(BF16), vec(final_g), tm=_tile_rows("gmlp"))
    return h
```

```python
import functools

import jax
import jax.numpy as jnp
from jax import lax
from jax.experimental import pallas as pl
from jax.experimental.pallas import tpu as pltpu

F32 = jnp.float32
BF16 = jnp.bfloat16

RMS_EPS = 1e-6
RWKV_HEAD = 64
RWKV_GN_EPS = 64e-5
RWKV_LORA = 64
DECAY_SCALE = 0.6065306597126334
HGRN_HEAD = 128
GMLP_CHUNK = 128
GMLP_GROUP = 128
CONV_WIDTH = 3

LANES = 128
SUBLANES = 8
V7X_VMEM_BYTES = 64 * 1024 * 1024
VMEM_LIMIT = V7X_VMEM_BYTES * 7 // 8

CHUNK = 64
HALF = CHUNK // 2
SUB_ROWS = 256
SCAN_GROUP_CHUNKS = 2


def _tile_rows(name):
    return {"rwkv_prep": 2 * SUB_ROWS, "rwkv_scan": 2 * SUB_ROWS, "hgrn": 4 * SUB_ROWS, "conv": 4 * SUB_ROWS,
            "gmlp": 4 * SUB_ROWS}[name]


def _mm(a, b):
    return jnp.dot(a.astype(BF16), b.astype(BF16), preferred_element_type=F32)


def _mm_nt(a, b):
    return lax.dot_general(a.astype(BF16), b.astype(BF16), (((1,), (1,)), ((), ())),
                           preferred_element_type=F32)


def _mm_tn(a, b):
    return lax.dot_general(a.astype(BF16), b.astype(BF16), (((0,), (0,)), ((), ())),
                           preferred_element_type=F32)


def _split2(x):
    hi = x.astype(BF16)
    lo = (x - hi.astype(F32)).astype(BF16)
    return hi, lo


def _head_sums(x, e, et2):
    return _head_sums_expand(_head_sums_reduce(x, e), et2)


def _head_sums_reduce(x, e):
    return jnp.dot(x.astype(BF16), e, preferred_element_type=F32)


def _head_sums_expand(s, et2):
    hi, lo = _split2(s)
    return jnp.dot(jnp.concatenate([hi, lo], axis=1), et2, preferred_element_type=F32)


def _cumsum_rows_fused(tri2, g):
    return jnp.dot(tri2, jnp.concatenate(_split2(g), axis=0), preferred_element_type=F32)


def _sigmoid(x):
    return 0.5 * jnp.tanh(0.5 * x) + 0.5


def _silu(x):
    return x * _sigmoid(x)


def _log1pexp_neg(z):
    return jnp.log(1.0 + jnp.exp(-z))


def _softplus(x):
    return jnp.maximum(x, 0.0) + _log1pexp_neg(jnp.abs(x))


def _rmsnorm(x, g):
    ms = jnp.mean(x * x, axis=-1, keepdims=True)
    return x * lax.rsqrt(ms + RMS_EPS) * g


def _iota2(shape, axis):
    return lax.broadcasted_iota(jnp.int32, shape, axis)


def _shift_rows(x, tail, n):
    rolled = pltpu.roll(x, n, 0)
    head = rolled[0:SUBLANES]
    row = _iota2(head.shape, 0)
    head = jnp.where(row < n, pltpu.roll(tail, n, 0), head)
    return jnp.concatenate([head, rolled[SUBLANES:]], axis=0)


def _tri_ones(n):
    return (_iota2((n, n), 0) >= _iota2((n, n), 1)).astype(BF16)


def _projected_ahead(h_ref, tm, project):
    subs = [pl.ds(s0, SUB_ROWS) for s0 in range(0, tm, SUB_ROWS)]
    x = h_ref[subs[0], :]
    ahead = (x, project(x))
    for i, rows in enumerate(subs):
        cur = ahead
        if i + 1 < len(subs):
            x = h_ref[subs[i + 1], :]
            ahead = (x, project(x))
        yield rows, cur[0], cur[1]


def _row_spec(tm, d):
    return pl.BlockSpec((None, tm, d), lambda b, t: (b, t, 0))


def _full_spec(shape):
    n = len(shape)
    return pl.BlockSpec(shape, lambda *_: (0,) * n)


def _params(n_axes):
    return pltpu.CompilerParams(dimension_semantics=("arbitrary",) * n_axes, vmem_limit_bytes=VMEM_LIMIT)


def _rwkv_prep_body(h_ref, ng_ref, win_ref, mu_ref, w0_ref, ww_ref, a0_ref, wa_ref, kk_ref, ka_ref,
                    e_ref, et_ref,
                    r_out, k_out, v_out, g_out, kk_out, bv_out, gate_out, carry_ref, *, d):
    tm = h_ref.shape[0]

    @pl.when(pl.program_id(1) == 0)
    def _():
        carry_ref[...] = jnp.zeros_like(carry_ref)

    for s0 in range(0, tm, SUB_ROWS):
        rows = pl.ds(s0, SUB_ROWS)
        hn = _rmsnorm(h_ref[rows, :], ng_ref[...]).astype(BF16)

        def project(lo, hi):
            p = jnp.dot(hn, win_ref[:, lo:hi], preferred_element_type=F32)
            prev = _shift_rows(p, carry_ref[:, lo:hi], 1)
            carry_ref[:, lo:hi] = p[SUB_ROWS - SUBLANES:SUB_ROWS, :]
            return p + (prev - p) * mu_ref[:, lo:hi]

        lw = project(3 * d, 3 * d + 2 * RWKV_LORA)
        k = project(d, 2 * d)
        lw = jnp.where(_iota2(lw.shape, 1) < RWKV_LORA, jnp.tanh(lw), lw)
        dw = _mm(lw, ww_ref[...])
        da = _mm(lw, wa_ref[...])
        r_out[rows, :] = project(0, d)
        kk = k * kk_ref[...]
        n2 = _head_sums_reduce(kk * kk, e_ref[...])
        v_out[rows, :] = project(2 * d, 3 * d)
        n2 = _head_sums_expand(n2, et_ref[...])
        gate_out[rows, :] = project(3 * d + 2 * RWKV_LORA, win_ref.shape[1])
        a = _sigmoid(a0_ref[...] + da)
        kk = kk * lax.rsqrt(jnp.maximum(n2, 1e-24))
        k_out[rows, :] = k * (1.0 + (a - 1.0) * ka_ref[...])
        g_out[rows, :] = -DECAY_SCALE * _sigmoid(w0_ref[...] + dw)
        kk_out[rows, :] = kk
        bv_out[rows, :] = kk * a


def _rwkv_prep(h, ng, w_in, mu, w0, ww_pad, a0, wa_pad, k_k, k_a, e, et, *, tm):
    bsz, t, d = h.shape
    cols = w_in.shape[1]
    row = _row_spec(tm, d)
    vec = _full_spec((1, d))
    out = jax.ShapeDtypeStruct((bsz, t, d), F32)
    return pl.pallas_call(
        functools.partial(_rwkv_prep_body, d=d),
        grid=(bsz, t // tm),
        in_specs=[row, vec, _full_spec(w_in.shape), _full_spec((1, cols)), vec, _full_spec(ww_pad.shape), vec,
                  _full_spec(wa_pad.shape), vec, vec, _full_spec(e.shape), _full_spec(et.shape)],
        out_specs=[row] * 7,
        out_shape=[out] * 7,
        scratch_shapes=[pltpu.VMEM((SUBLANES, cols), F32)],
        compiler_params=_params(2),
        name="rwkv_prep",
    )(h, ng, w_in, mu, w0, ww_pad, a0, wa_pad, k_k, k_a, e, et)


def _stack_heads(x, m0):
    return jnp.concatenate([jnp.where(m0, x, 0.0), jnp.where(m0, 0.0, x)], axis=0)


def _unit_lower_inverse(lows, i2, j2):
    s = 1
    eye = jnp.where(i2 == j2, 1.0, 0.0)
    t_inv = [eye for _ in lows]
    while s < CHUNK:
        keep = ((i2 // (2 * s)) == (j2 // (2 * s))) & ((i2 // s) != (j2 // s))
        subs = [jnp.where(keep, low, 0.0).astype(BF16) for low in lows]
        if s == 1:
            t_inv = [t + sub for t, sub in zip(t_inv, subs)]
        else:
            t_bf = [t.astype(BF16) for t in t_inv]
            left = [_mm(tb, sub) for tb, sub in zip(t_bf, subs)]
            t_inv = [t + _mm(lt, tb) for t, lt, tb in zip(t_inv, left, t_bf)]
        s *= 2
    return t_inv


def _rwkv_scan_body(*refs, n_cast):
    (r_ref, k_ref, v_ref, g_ref, kk_ref, bv_ref, gate_ref, h_ref, rk_ref, gg_ref, gb_ref,
     wout_ref, e_ref, et_ref) = refs[:14]
    cast_in, o_ref, cast_out = refs[14:14 + n_cast], refs[14 + n_cast], refs[15 + n_cast:15 + 2 * n_cast]
    state_ref, y_ref = refs[15 + 2 * n_cast:]
    for src, dst in zip(cast_in, cast_out):
        dst[...] = src[...].astype(BF16)
    tm = r_ref.shape[0]
    c = CHUNK
    w = 2 * RWKV_HEAD
    nc = tm // c
    n_pairs = r_ref.shape[1] // w

    @pl.when(pl.program_id(2) == 0)
    def _():
        state_ref[...] = jnp.zeros_like(state_ref)

    m0 = _iota2((c, w), 1) < RWKV_HEAD
    tri2 = jnp.concatenate([_tri_ones(c)] * 2, axis=1)
    i2 = _iota2((2 * c, w), 0)
    j2 = _iota2((2 * c, w), 1)
    same_head = (i2 // RWKV_HEAD) == (j2 // RWKV_HEAD)
    ir = _iota2((c, w), 0)
    jr = _iota2((c, w), 1) % c
    ir2 = _iota2((c, 2 * w), 0)
    jr2 = _iota2((c, 2 * w), 1) % c
    stack = functools.partial(_stack_heads, m0=m0)
    each = lambda f, *lists: [f(*xs) for xs in zip(*lists)]
    h_state = [state_ref[pi] for pi in range(n_pairs)]
    for c0 in range(0, nc, SCAN_GROUP_CHUNKS):
        where = [(pl.ds(ci * c, c), pl.ds(pi * w, w)) for ci in range(c0, min(c0 + SCAN_GROUP_CHUNKS, nc))
                 for pi in range(n_pairs)]
        load = lambda ref: [ref[rows, cols] for rows, cols in where]

        r, k, v, g, kk, bv = load(r_ref), load(k_ref), load(v_ref), load(g_ref), load(kk_ref), load(bv_ref)
        bc = each(lambda x: _cumsum_rows_fused(tri2, x), g)
        bl = each(lambda x: x[c - 1:c, :], bc)
        p_inv = each(lambda x: jnp.exp(-x), bc)
        p_end = each(lambda x, y: jnp.exp(x - y), bl, bc)
        rt = each(lambda a, x: a * jnp.exp(x), r, bc)
        at2 = each(lambda a, x, y: stack(-a * jnp.exp(x - y)).astype(BF16), kk, bc, g)
        v2 = each(lambda a: stack(a).astype(BF16), v)
        lhs = each(lambda a, x: jnp.concatenate([a, x.astype(BF16)], axis=0), at2, rt)
        rhs = each(lambda a, x, p: jnp.concatenate([stack(a * p), stack(x * p)], axis=0).astype(BF16), bv, k, p_inv)
        end = each(lambda a, x, p: jnp.concatenate([a * p, x * p], axis=0).astype(BF16), bv, k, p_end)

        prod = each(_mm_nt, lhs, rhs)
        low = each(lambda x: jnp.where(j2 < i2, x[0:2 * c, 0:w], 0.0), prod)
        a_ak = each(lambda x: jnp.where(jr < ir, x[0:c, w:2 * w] + x[c:2 * c, w:2 * w], 0.0).astype(BF16), prod)
        a_r = each(lambda x: jnp.where(jr2 <= ir2, x[2 * c:3 * c, :], 0.0).astype(BF16), prod)
        t_inv = _unit_lower_inverse(low, i2, j2)
        t_row = each(lambda t: (t[0:c, :] + t[c:2 * c, :]).astype(BF16), t_inv)
        z = each(_mm, a_ak, v2)
        w12 = each(lambda t, a, x: _mm(t, jnp.concatenate([a, stack(x).astype(BF16)], axis=1)), t_row, at2, z)
        w12b = each(lambda x: x.astype(BF16), w12)
        q12 = each(lambda a, x: _mm(a[:, 0:w], jnp.concatenate(
            [_stack_heads(x[:, 0:w], m0), _stack_heads(x[:, w:2 * w], m0)], axis=1)), a_r, w12b)
        q1 = each(lambda a, x: a + x[:, 0:w], rt, q12)
        q2 = each(lambda x, a, y: x[:, w:2 * w] + _mm(a[:, w:2 * w], y), q12, a_r, v2)
        zero_c = jnp.zeros((c, w), BF16)
        ti = each(lambda e, x, y: _mm_tn(e, jnp.concatenate(
            [x, jnp.concatenate([zero_c, y.astype(BF16)], axis=1)], axis=0)), end, w12b, v)
        trans = each(lambda x, e: jnp.where(same_head, x[:, 0:w], 0.0)
                     + jnp.where(i2 == j2, jnp.broadcast_to(jnp.exp(e), (2 * c, w)), 0.0), ti, bl)
        inject = each(lambda x: jnp.where(same_head, x[:, w:2 * w], 0.0), ti)
        q1t = each(lambda a, x: jnp.concatenate([a, x], axis=0).astype(BF16), q1, trans)

        for n, (rows, cols) in enumerate(where):
            pi = n % n_pairs
            yh = _mm(q1t[n], h_state[pi])
            y_ref[rows, cols] = yh[0:c, :] + q2[n]
            h_state[pi] = yh[c:3 * c, :] + inject[n]
    for pi in range(n_pairs):
        state_ref[pi] = h_state[pi]
    _rwkv_out_tile(h_ref, y_ref, r_ref, k_ref, v_ref, gate_ref, rk_ref, gg_ref, gb_ref, wout_ref, e_ref, et_ref, o_ref)


def _rwkv_scan(h, r, k, v, g, kk, bv, gate, r_k, gn_g, gn_b, w_out, e, et, to_cast, *, tm):
    bsz, t, d = r.shape
    w = 2 * RWKV_HEAD
    nt = t // tm
    row = pl.BlockSpec((None, tm, d), lambda b, p, i: (b, i, 0))
    vec = _full_spec((1, d))
    slabs, cast_shapes = [], []
    for z in to_cast:
        rows = z.shape[1] // (bsz * nt)
        assert rows * bsz * nt == z.shape[1] and rows % (2 * SUBLANES) == 0, z.shape
        slabs.append(pl.BlockSpec((None, rows, z.shape[2]), lambda b, p, i: (0, b * nt + i, 0)))
        cast_shapes.append(jax.ShapeDtypeStruct(z.shape, BF16))
    out = pl.pallas_call(
        functools.partial(_rwkv_scan_body, n_cast=len(to_cast)),
        grid=(bsz, 1, nt),
        in_specs=([row] * 8 + [vec, vec, vec, _full_spec(w_out.shape), _full_spec(e.shape), _full_spec(et.shape)]
                  + slabs),
        out_specs=[row] + slabs,
        out_shape=[jax.ShapeDtypeStruct((bsz, t, d), F32)] + cast_shapes,
        scratch_shapes=[pltpu.VMEM((d // w, w, w), F32), pltpu.VMEM((tm, d), F32)],
        compiler_params=_params(3),
        name="rwkv_scan",
    )(r, k, v, g, kk, bv, gate, h, r_k, gn_g, gn_b, w_out, e, et, *to_cast)
    return out[0], out[1:]


def _rwkv_out_tile(h_ref, y_ref, r_ref, k_ref, v_ref, gate_ref, rk_ref, gg_ref, gb_ref, wout_ref,
                   e_ref, et_ref, o_ref):
    tm = h_ref.shape[0]
    subs = [pl.ds(s0, SUB_ROWS) for s0 in range(0, tm, SUB_ROWS)]
    each = lambda f, *lists: [f(*xs) for xs in zip(*lists)]
    reduce_ = lambda z: _head_sums_reduce(z, e_ref[...])
    expand = lambda s: _head_sums_expand(s, et_ref[...])
    y = [y_ref[rows, :] for rows in subs]
    rk = [r_ref[rows, :] * k_ref[rows, :] * rk_ref[...] for rows in subs]
    y_sum, rk_sum = each(reduce_, y), each(reduce_, rk)
    y_sum, rk_sum = each(expand, y_sum), each(expand, rk_sum)
    dy = each(lambda a, s: a - s * (1.0 / RWKV_HEAD), y, y_sum)
    var = each(expand, each(lambda a: reduce_(a * a), dy))
    for rows, dy_, var_, rk_ in zip(subs, dy, var, rk_sum):
        yn = dy_ * lax.rsqrt(var_ * (1.0 / RWKV_HEAD) + RWKV_GN_EPS) * gg_ref[...] + gb_ref[...]
        out = (yn + rk_ * v_ref[rows, :]) * _silu(gate_ref[rows, :])
        o_ref[rows, :] = h_ref[rows, :] + _mm(out, wout_ref[...])


def _hgrn_body(h_ref, ng_ref, win_ref, lbl_ref, gg_ref, wout_ref, o_ref, state_ref, *, layer):
    tm, d = h_ref.shape
    c = CHUNK
    nh = d // HGRN_HEAD

    @pl.when(pl.program_id(1) == 0)
    def _():
        state_ref[...] = jnp.zeros_like(state_ref)

    logits = lbl_ref[...]
    ex = jnp.exp(logits - jnp.max(logits, axis=0, keepdims=True))
    lb = jnp.sum(ex[1:layer + 1, :], axis=0, keepdims=True) / jnp.sum(ex, axis=0, keepdims=True)

    nc = SUB_ROWS // c
    tri2 = jnp.concatenate([_tri_ones(c)] * 2, axis=1)
    row = _iota2((c, HGRN_HEAD), 0)
    top = row < HALF
    ri, ci_ = _iota2((c, c), 0), _iota2((c, c), 1)
    diag_blocks = (ri // HALF) == (ci_ // HALF)
    causal = ri >= ci_
    each = lambda f, *lists: [f(*xs) for xs in zip(*lists)]
    where = [(slice(ci * c, (ci + 1) * c), slice(hi * HGRN_HEAD, (hi + 1) * HGRN_HEAD))
             for ci in range(nc) for hi in range(nh)]
    take = lambda z: [z[rows, cols] for rows, cols in where]
    s_t = [state_ref[hi] for hi in range(nh)]

    project = lambda x: _mm(_rmsnorm(x, ng_ref[...]), win_ref[...])
    for sub_rows, x, p in _projected_ahead(h_ref, tm, project):
        q = p[:, 0:d]
        f_pre = p[:, d:2 * d]
        v = p[:, 2 * d:3 * d]
        gate = p[:, 3 * d:4 * d]
        sig = _sigmoid(f_pre)
        log_f = jnp.log(lb + (1.0 - lb) * sig)
        k = (1.0 - lb) * (1.0 - sig)
        qs, ks, vs, gs = take(q), take(k), take(v), take(log_f)
        b = each(lambda x: _cumsum_rows_fused(tri2, x), gs)
        bl = each(lambda x: x[c - 1:c], b)
        beta = each(lambda x: x[HALF - 1:HALF], b)
        m_a = each(lambda x: x[HALF // 2:HALF // 2 + 1], b)
        m_b = each(lambda x: x[HALF + HALF // 2:HALF + HALF // 2 + 1], b)
        mid = each(lambda x, y: jnp.where(top, x, y), m_a, m_b)
        q_df = each(lambda a, x, m: a * jnp.exp(x - m), qs, b, mid)
        k_df = each(lambda a, x, m: a * jnp.exp(m - x), ks, b, mid)
        q_d = each(lambda a: a.astype(BF16), q_df)
        k_d = each(lambda a: a.astype(BF16), k_df)
        q_o = each(lambda a, y, e: jnp.where(top, 0.0, a * jnp.exp(y - e)).astype(BF16), q_df, m_b, beta)
        k_o = each(lambda a, x, e: jnp.where(top, a * jnp.exp(e - x), 0.0).astype(BF16), k_df, m_a, beta)
        q_in = each(lambda a, x, y: (a * jnp.where(top, jnp.exp(x), jnp.exp(y))).astype(BF16),
                    q_df, m_a, m_b)
        k_out = each(lambda a, x, y, e: (a * jnp.where(top, jnp.exp(e - x), jnp.exp(e - y))).astype(BF16),
                     k_df, m_a, m_b, bl)
        v_b = each(lambda a: a.astype(BF16), vs)
        att = each(lambda qd, kd, qo, ko: jnp.where(diag_blocks, jnp.where(causal, _mm_nt(qd, kd), 0.0), _mm_nt(qo, ko)),
                   q_d, k_d, q_o, k_o)
        o_intra = each(_mm, att, v_b)
        kv = each(_mm_tn, v_b, k_out)
        s_in = []
        for n in range(len(where)):
            hi = n % nh
            s_in.append(s_t[hi])
            s_t[hi] = s_t[hi] * jnp.exp(bl[n]) + kv[n]
        o = each(lambda oi, a, s: oi + _mm_nt(a, s), o_intra, q_in, s_in)
        o = each(lambda z: z * lax.rsqrt(jnp.mean(z * z, axis=-1, keepdims=True) + RMS_EPS), o)
        o_all = jnp.concatenate([jnp.concatenate(o[ci * nh:(ci + 1) * nh], axis=1) for ci in range(nc)], axis=0)
        o_all = o_all * gg_ref[...] * _silu(gate)
        o_ref[sub_rows, :] = x + _mm(o_all, wout_ref[...])
    for hi in range(nh):
        state_ref[hi] = s_t[hi]


def _hgrn_layer(h, ng, w_in, lb_logits, gn_g, w_out, *, layer, tm):
    bsz, t, d = h.shape
    row = _row_spec(tm, d)
    vec = _full_spec((1, d))
    return pl.pallas_call(
        functools.partial(_hgrn_body, layer=layer),
        grid=(bsz, t // tm),
        in_specs=[row, vec, _full_spec(w_in.shape), _full_spec(lb_logits.shape), vec, _full_spec(w_out.shape)],
        out_specs=row,
        out_shape=jax.ShapeDtypeStruct((bsz, t, d), F32),
        scratch_shapes=[pltpu.VMEM((d // HGRN_HEAD, HGRN_HEAD, HGRN_HEAD), F32)],
        compiler_params=_params(2),
        name="hgrn_layer",
    )(h, ng, w_in, lb_logits, gn_g, w_out)


def _conv_body(h_ref, ng_ref, win_ref, cw_ref, wout_ref, o_ref, carry_ref):
    tm, d = h_ref.shape

    @pl.when(pl.program_id(1) == 0)
    def _():
        carry_ref[...] = jnp.zeros_like(carry_ref)

    project = lambda x: _mm(_rmsnorm(x, ng_ref[...]), win_ref[...])
    tail = carry_ref[...]
    for rows, x, p in _projected_ahead(h_ref, tm, project):
        b_gate = p[:, 0:d]
        y = p[:, d:2 * d] * p[:, 2 * d:3 * d]
        gate = p[:, 3 * d:4 * d]
        y1 = _shift_rows(y, tail, 1)
        y2 = _shift_rows(y, tail, 2)
        tail = y[SUB_ROWS - SUBLANES:SUB_ROWS, :]
        yc = cw_ref[0:1, :] * y2 + cw_ref[1:2, :] * y1 + cw_ref[2:3, :] * y
        o_ref[rows, :] = x + _mm(b_gate * yc * _silu(gate), wout_ref[...])
    carry_ref[...] = tail


def _conv_layer(h, ng, w_in, conv_w, w_out, *, tm):
    bsz, t, d = h.shape
    row = _row_spec(tm, d)
    return pl.pallas_call(
        _conv_body,
        grid=(bsz, t // tm),
        in_specs=[row, _full_spec((1, d)), _full_spec(w_in.shape), _full_spec(conv_w.shape), _full_spec(w_out.shape)],
        out_specs=row,
        out_shape=jax.ShapeDtypeStruct((bsz, t, d), F32),
        scratch_shapes=[pltpu.VMEM((SUBLANES, d), F32)],
        compiler_params=_params(2),
        name="conv_layer",
    )(h, ng, w_in, conv_w, w_out)


def _gmlp_body(h_ref, ng_ref, win_ref, vg_ref, ws_ref, bs_ref, wout_ref, fg_ref, o_ref):
    tm, d = h_ref.shape
    c = GMLP_CHUNK
    causal = _iota2((c, c), 0) >= _iota2((c, c), 1)
    bias = bs_ref[...]
    nc = SUB_ROWS // c
    project = lambda x: _mm(_rmsnorm(x, ng_ref[...]), win_ref[...])
    for rows, x, p in _projected_ahead(h_ref, tm, project):
        u = p[:, 0:d]
        vb = _rmsnorm(p[:, d:2 * d], vg_ref[...]).astype(BF16)
        gate = p[:, 2 * d:3 * d]
        parts = []
        for gi in range(d // GMLP_GROUP):
            cols = slice(gi * GMLP_GROUP, (gi + 1) * GMLP_GROUP)
            v_g = jnp.concatenate([vb[ci * c:(ci + 1) * c, cols] for ci in range(nc)], axis=1)
            parts.append(_mm(jnp.where(causal, ws_ref[gi], 0.0), v_g))
        s = jnp.concatenate(
            [jnp.concatenate([q[:, ci * GMLP_GROUP:(ci + 1) * GMLP_GROUP] for q in parts], axis=1) + bias
             for ci in range(nc)], axis=0)
        out = x + _mm(u * s * _silu(gate), wout_ref[...])
        o_ref[rows, :] = _rmsnorm(out, fg_ref[...])


def _gmlp_layer(h, ng, w_in, v_g, w_s, bs_full, w_out, final_g, *, tm):
    bsz, t, d = h.shape
    row = _row_spec(tm, d)
    vec = _full_spec((1, d))
    return pl.pallas_call(
        _gmlp_body,
        grid=(bsz, t // tm),
        in_specs=[row, vec, _full_spec(w_in.shape), vec, _full_spec(w_s.shape), _full_spec(bs_full.shape),
                  _full_spec(w_out.shape), vec],
        out_specs=row,
        out_shape=jax.ShapeDtypeStruct((bsz, t, d), F32),
        compiler_params=_params(2),
        name="gmlp_layer",
    )(h, ng, w_in, v_g, w_s, bs_full, w_out, final_g)


def _rwkv_layer(h, ng, w_in, mu, w0, w_w2, a0, w_a2, k_k, k_a, r_k, gn_g, gn_b, w_out, to_cast=()):
    d = h.shape[-1]
    zeros = jnp.zeros((RWKV_LORA, d), F32)
    ww_pad = jnp.concatenate([w_w2, zeros], axis=0).astype(BF16)
    wa_pad = jnp.concatenate([zeros, w_a2], axis=0).astype(BF16)
    e = (jnp.arange(d)[:, None] // RWKV_HEAD == jnp.arange(LANES)[None, :]).astype(BF16)
    et = jnp.concatenate([e.T, e.T], axis=0)
    vec = lambda z: z.reshape(1, -1)
    r, k, v, g, kk, bv, gate = _rwkv_prep(h, vec(ng), w_in.astype(BF16), vec(mu), vec(w0), ww_pad, vec(a0), wa_pad,
                                          vec(k_k), vec(k_a), e, et, tm=_tile_rows("rwkv_prep"))
    return _rwkv_scan(h, r, k, v, g, kk, bv, gate, vec(r_k), vec(gn_g), vec(gn_b), w_out.astype(BF16), e, et,
                      list(to_cast), tm=_tile_rows("rwkv_scan"))


def kernel(x, norm_g, final_g, rwkv_w_in, rwkv_mu, rwkv_w0, rwkv_w_w2, rwkv_a0, rwkv_w_a2, rwkv_k_k, rwkv_k_a, rwkv_r_k, rwkv_gn_g, rwkv_gn_b, rwkv_w_out, hgrn_lb_logits, hgrn_w_in, hgrn_gn_g, hgrn_w_out, conv_w_in, conv_w, conv_w_out, gmlp_w_in, gmlp_v_g, gmlp_w_s, gmlp_b_s, gmlp_w_out):
    depth = norm_g.shape[0]
    assert depth == 4, "the fused final norm assumes the gMLP layer is the last one"
    vec = lambda z: z.reshape(1, -1)
    h = x
    later = (hgrn_w_in, hgrn_w_out, conv_w_in, conv_w_out, gmlp_w_in, gmlp_w_out)
    h, later = _rwkv_layer(h, norm_g[0], rwkv_w_in[0], rwkv_mu[0], rwkv_w0[0], rwkv_w_w2[0], rwkv_a0[0], rwkv_w_a2[0],
                           rwkv_k_k[0], rwkv_k_a[0], rwkv_r_k[0], rwkv_gn_g[0], rwkv_gn_b[0], rwkv_w_out[0],
                           to_cast=later)
    hgrn_in, hgrn_out, conv_in, conv_out, gmlp_in, gmlp_out = (z[0] for z in later)
    h = _hgrn_layer(h, vec(norm_g[1]), hgrn_in, hgrn_lb_logits, vec(hgrn_gn_g[0]), hgrn_out,
                    layer=1, tm=_tile_rows("hgrn"))
    h = _conv_layer(h, vec(norm_g[2]), conv_in, conv_w[0], conv_out, tm=_tile_rows("conv"))
    bs_full = jnp.repeat(gmlp_b_s[0].T, GMLP_GROUP, axis=1)
    h = _gmlp_layer(h, vec(norm_g[3]), gmlp_in, vec(gmlp_v_g[0]), gmlp_w_s[0].astype(BF16),
                    bs_full, gmlp_out, vec(final_g), tm=_tile_rows("gmlp"))
    return h
```

```python
import functools

import jax
import jax.numpy as jnp
from jax import lax
from jax.experimental import pallas as pl
from jax.experimental.pallas import tpu as pltpu

F32 = jnp.float32
BF16 = jnp.bfloat16

RMS_EPS = 1e-6
RWKV_HEAD = 64
RWKV_GN_EPS = 64e-5
RWKV_LORA = 64
DECAY_SCALE = 0.6065306597126334
HGRN_HEAD = 128
GMLP_CHUNK = 128
GMLP_GROUP = 128

LANES = 128
SUBLANES = 8
V7X_VMEM_BYTES = 64 * 1024 * 1024
VMEM_LIMIT = V7X_VMEM_BYTES * 7 // 8

CHUNK = 64
HALF = CHUNK // 2
SUB_ROWS = 256
SCAN_GROUP_CHUNKS = 2


def _tile_rows(name):
    return {"rwkv_prep": 2 * SUB_ROWS, "rwkv_scan": 2 * SUB_ROWS, "hgrn": 4 * SUB_ROWS, "conv": 4 * SUB_ROWS,
            "gmlp": 4 * SUB_ROWS}[name]


def _mm(a, b):
    return jnp.dot(a.astype(BF16), b.astype(BF16), preferred_element_type=F32)


def _mm_nt(a, b):
    return lax.dot_general(a.astype(BF16), b.astype(BF16), (((1,), (1,)), ((), ())),
                           preferred_element_type=F32)


def _mm_tn(a, b):
    return lax.dot_general(a.astype(BF16), b.astype(BF16), (((0,), (0,)), ((), ())),
                           preferred_element_type=F32)


def _split2(x):
    hi = x.astype(BF16)
    lo = (x - hi.astype(F32)).astype(BF16)
    return hi, lo


def _head_sums_reduce(x, e):
    return jnp.dot(x.astype(BF16), e, preferred_element_type=F32)


def _head_sums_expand(s, et2):
    hi, lo = _split2(s)
    return jnp.dot(jnp.concatenate([hi, lo], axis=1), et2, preferred_element_type=F32)


def _cumsum_rows_fused(tri2, g):
    return jnp.dot(tri2, jnp.concatenate(_split2(g), axis=0), preferred_element_type=F32)


def _sigmoid(x):
    return 0.5 * jnp.tanh(0.5 * x) + 0.5


def _silu(x):
    return x * _sigmoid(x)


def _rmsnorm(x, g):
    ms = jnp.mean(x * x, axis=-1, keepdims=True)
    return x * lax.rsqrt(ms + RMS_EPS) * g


def _iota2(shape, axis):
    return lax.broadcasted_iota(jnp.int32, shape, axis)


def _shift_rows(x, tail, n):
    rolled = pltpu.roll(x, n, 0)
    head = rolled[0:SUBLANES]
    row = _iota2(head.shape, 0)
    head = jnp.where(row < n, pltpu.roll(tail, n, 0), head)
    return jnp.concatenate([head, rolled[SUBLANES:]], axis=0)


def _tri_ones(n):
    return (_iota2((n, n), 0) >= _iota2((n, n), 1)).astype(BF16)


def _projected_ahead(h_ref, tm, project):
    subs = [pl.ds(s0, SUB_ROWS) for s0 in range(0, tm, SUB_ROWS)]
    x = h_ref[subs[0], :]
    ahead = (x, project(x))
    for i, rows in enumerate(subs):
        cur = ahead
        if i + 1 < len(subs):
            x = h_ref[subs[i + 1], :]
            ahead = (x, project(x))
        yield rows, cur[0], cur[1]


def _row_spec(tm, d):
    return pl.BlockSpec((None, tm, d), lambda b, t: (b, t, 0))


def _full_spec(shape):
    n = len(shape)
    return pl.BlockSpec(shape, lambda *_: (0,) * n)


def _params(n_axes):
    return pltpu.CompilerParams(dimension_semantics=("arbitrary",) * n_axes, vmem_limit_bytes=VMEM_LIMIT)


def _rwkv_prep_body(h_ref, ng_ref, win_ref, mu_ref, w0_ref, ww_ref, a0_ref, wa_ref, kk_ref, ka_ref,
                    e_ref, et_ref,
                    r_out, k_out, v_out, g_out, kk_out, bv_out, gate_out, carry_ref, *, d):
    tm = h_ref.shape[0]

    @pl.when(pl.program_id(1) == 0)
    def _():
        carry_ref[...] = jnp.zeros_like(carry_ref)

    for s0 in range(0, tm, SUB_ROWS):
        rows = pl.ds(s0, SUB_ROWS)
        hn = _rmsnorm(h_ref[rows, :], ng_ref[...]).astype(BF16)

        def project(lo, hi):
            p = jnp.dot(hn, win_ref[:, lo:hi], preferred_element_type=F32)
            prev = _shift_rows(p, carry_ref[:, lo:hi], 1)
            carry_ref[:, lo:hi] = p[SUB_ROWS - SUBLANES:SUB_ROWS, :]
            return p + (prev - p) * mu_ref[:, lo:hi]

        lw = project(3 * d, 3 * d + 2 * RWKV_LORA)
        k = project(d, 2 * d)
        lw = jnp.where(_iota2(lw.shape, 1) < RWKV_LORA, jnp.tanh(lw), lw)
        dw = _mm(lw, ww_ref[...])
        da = _mm(lw, wa_ref[...])
        r_out[rows, :] = project(0, d)
        kk = k * kk_ref[...]
        n2 = _head_sums_reduce(kk * kk, e_ref[...])
        v_out[rows, :] = project(2 * d, 3 * d)
        n2 = _head_sums_expand(n2, et_ref[...])
        gate_out[rows, :] = project(3 * d + 2 * RWKV_LORA, win_ref.shape[1])
        a = _sigmoid(a0_ref[...] + da)
        kk = kk * lax.rsqrt(jnp.maximum(n2, 1e-24))
        k_out[rows, :] = k * (1.0 + (a - 1.0) * ka_ref[...])
        g_out[rows, :] = -DECAY_SCALE * _sigmoid(w0_ref[...] + dw)
        kk_out[rows, :] = kk
        bv_out[rows, :] = kk * a


def _rwkv_prep(h, ng, w_in, mu, w0, ww_pad, a0, wa_pad, k_k, k_a, e, et, *, tm):
    bsz, t, d = h.shape
    cols = w_in.shape[1]
    row = _row_spec(tm, d)
    vec = _full_spec((1, d))
    out = jax.ShapeDtypeStruct((bsz, t, d), F32)
    return pl.pallas_call(
        functools.partial(_rwkv_prep_body, d=d),
        grid=(bsz, t // tm),
        in_specs=[row, vec, _full_spec(w_in.shape), _full_spec((1, cols)), vec, _full_spec(ww_pad.shape), vec,
                  _full_spec(wa_pad.shape), vec, vec, _full_spec(e.shape), _full_spec(et.shape)],
        out_specs=[row] * 7,
        out_shape=[out] * 7,
        scratch_shapes=[pltpu.VMEM((SUBLANES, cols), F32)],
        compiler_params=_params(2),
        name="rwkv_prep",
    )(h, ng, w_in, mu, w0, ww_pad, a0, wa_pad, k_k, k_a, e, et)


def _stack_heads(x, m0):
    return jnp.concatenate([jnp.where(m0, x, 0.0), jnp.where(m0, 0.0, x)], axis=0)


def _unit_lower_inverse(lows, i2, j2):
    s = 1
    eye = jnp.where(i2 == j2, 1.0, 0.0)
    t_inv = [eye for _ in lows]
    while s < CHUNK:
        keep = ((i2 // (2 * s)) == (j2 // (2 * s))) & ((i2 // s) != (j2 // s))
        subs = [jnp.where(keep, low, 0.0).astype(BF16) for low in lows]
        if s == 1:
            t_inv = [t + sub for t, sub in zip(t_inv, subs)]
        else:
            t_bf = [t.astype(BF16) for t in t_inv]
            left = [_mm(tb, sub) for tb, sub in zip(t_bf, subs)]
            t_inv = [t + _mm(lt, tb) for t, lt, tb in zip(t_inv, left, t_bf)]
        s *= 2
    return t_inv


def _rwkv_scan_body(*refs, n_cast):
    (r_ref, k_ref, v_ref, g_ref, kk_ref, bv_ref, gate_ref, h_ref, rk_ref, gg_ref, gb_ref,
     wout_ref, e_ref, et_ref) = refs[:14]
    cast_in, o_ref, cast_out = refs[14:14 + n_cast], refs[14 + n_cast], refs[15 + n_cast:15 + 2 * n_cast]
    state_ref, y_ref = refs[15 + 2 * n_cast:]
    for src, dst in zip(cast_in, cast_out):
        dst[...] = src[...].astype(BF16)
    tm = r_ref.shape[0]
    c = CHUNK
    w = 2 * RWKV_HEAD
    nc = tm // c
    n_pairs = r_ref.shape[1] // w

    @pl.when(pl.program_id(2) == 0)
    def _():
        state_ref[...] = jnp.zeros_like(state_ref)

    m0 = _iota2((c, w), 1) < RWKV_HEAD
    tri2 = jnp.concatenate([_tri_ones(c)] * 2, axis=1)
    i2 = _iota2((2 * c, w), 0)
    j2 = _iota2((2 * c, w), 1)
    same_head = (i2 // RWKV_HEAD) == (j2 // RWKV_HEAD)
    ir = _iota2((c, w), 0)
    jr = _iota2((c, w), 1) % c
    ir2 = _iota2((c, 2 * w), 0)
    jr2 = _iota2((c, 2 * w), 1) % c
    stack = functools.partial(_stack_heads, m0=m0)
    each = lambda f, *lists: [f(*xs) for xs in zip(*lists)]
    h_state = [state_ref[pi] for pi in range(n_pairs)]
    for c0 in range(0, nc, SCAN_GROUP_CHUNKS):
        where = [(pl.ds(ci * c, c), pl.ds(pi * w, w)) for ci in range(c0, min(c0 + SCAN_GROUP_CHUNKS, nc))
                 for pi in range(n_pairs)]
        load = lambda ref: [ref[rows, cols] for rows, cols in where]

        r, k, v, g, kk, bv = load(r_ref), load(k_ref), load(v_ref), load(g_ref), load(kk_ref), load(bv_ref)
        bc = each(lambda x: _cumsum_rows_fused(tri2, x), g)
        bl = each(lambda x: x[c - 1:c, :], bc)
        p_inv = each(lambda x: jnp.exp(-x), bc)
        p_end = each(lambda x, y: jnp.exp(x - y), bl, bc)
        rt = each(lambda a, x: a * jnp.exp(x), r, bc)
        at2 = each(lambda a, x, y: stack(-a * jnp.exp(x - y)).astype(BF16), kk, bc, g)
        v2 = each(lambda a: stack(a).astype(BF16), v)
        lhs = each(lambda a, x: jnp.concatenate([a, x.astype(BF16)], axis=0), at2, rt)
        rhs = each(lambda a, x, p: jnp.concatenate([stack(a * p), stack(x * p)], axis=0).astype(BF16), bv, k, p_inv)
        end = each(lambda a, x, p: jnp.concatenate([a * p, x * p], axis=0).astype(BF16), bv, k, p_end)

        prod = each(_mm_nt, lhs, rhs)
        low = each(lambda x: jnp.where(j2 < i2, x[0:2 * c, 0:w], 0.0), prod)
        a_ak = each(lambda x: jnp.where(jr < ir, x[0:c, w:2 * w] + x[c:2 * c, w:2 * w], 0.0).astype(BF16), prod)
        a_r = each(lambda x: jnp.where(jr2 <= ir2, x[2 * c:3 * c, :], 0.0).astype(BF16), prod)
        t_inv = _unit_lower_inverse(low, i2, j2)
        t_row = each(lambda t: (t[0:c, :] + t[c:2 * c, :]).astype(BF16), t_inv)
        z = each(_mm, a_ak, v2)
        w12 = each(lambda t, a, x: _mm(t, jnp.concatenate([a, stack(x).astype(BF16)], axis=1)), t_row, at2, z)
        w12b = each(lambda x: x.astype(BF16), w12)
        q12 = each(lambda a, x: _mm(a[:, 0:w], jnp.concatenate(
            [_stack_heads(x[:, 0:w], m0), _stack_heads(x[:, w:2 * w], m0)], axis=1)), a_r, w12b)
        q1 = each(lambda a, x: a + x[:, 0:w], rt, q12)
        q2 = each(lambda x, a, y: x[:, w:2 * w] + _mm(a[:, w:2 * w], y), q12, a_r, v2)
        zero_c = jnp.zeros((c, w), BF16)
        ti = each(lambda e, x, y: _mm_tn(e, jnp.concatenate(
            [x, jnp.concatenate([zero_c, y.astype(BF16)], axis=1)], axis=0)), end, w12b, v)
        trans = each(lambda x, e: jnp.where(same_head, x[:, 0:w], 0.0)
                     + jnp.where(i2 == j2, jnp.broadcast_to(jnp.exp(e), (2 * c, w)), 0.0), ti, bl)
        inject = each(lambda x: jnp.where(same_head, x[:, w:2 * w], 0.0), ti)
        q1t = each(lambda a, x: jnp.concatenate([a, x], axis=0).astype(BF16), q1, trans)

        for n, (rows, cols) in enumerate(where):
            pi = n % n_pairs
            yh = _mm(q1t[n], h_state[pi])
            y_ref[rows, cols] = yh[0:c, :] + q2[n]
            h_state[pi] = yh[c:3 * c, :] + inject[n]
    for pi in range(n_pairs):
        state_ref[pi] = h_state[pi]
    _rwkv_out_tile(h_ref, y_ref, r_ref, k_ref, v_ref, gate_ref, rk_ref, gg_ref, gb_ref, wout_ref, e_ref, et_ref, o_ref)


def _rwkv_scan(h, r, k, v, g, kk, bv, gate, r_k, gn_g, gn_b, w_out, e, et, to_cast, *, tm):
    bsz, t, d = r.shape
    w = 2 * RWKV_HEAD
    nt = t // tm
    row = pl.BlockSpec((None, tm, d), lambda b, p, i: (b, i, 0))
    vec = _full_spec((1, d))
    slabs, cast_shapes = [], []
    for z in to_cast:
        rows = z.shape[1] // (bsz * nt)
        assert rows * bsz * nt == z.shape[1] and rows % (2 * SUBLANES) == 0, z.shape
        slabs.append(pl.BlockSpec((None, rows, z.shape[2]), lambda b, p, i: (0, b * nt + i, 0)))
        cast_shapes.append(jax.ShapeDtypeStruct(z.shape, BF16))
    out = pl.pallas_call(
        functools.partial(_rwkv_scan_body, n_cast=len(to_cast)),
        grid=(bsz, 1, nt),
        in_specs=([row] * 8 + [vec, vec, vec, _full_spec(w_out.shape), _full_spec(e.shape), _full_spec(et.shape)]
                  + slabs),
        out_specs=[row] + slabs,
        out_shape=[jax.ShapeDtypeStruct((bsz, t, d), F32)] + cast_shapes,
        scratch_shapes=[pltpu.VMEM((d // w, w, w), F32), pltpu.VMEM((tm, d), F32)],
        compiler_params=_params(3),
        name="rwkv_scan",
    )(r, k, v, g, kk, bv, gate, h, r_k, gn_g, gn_b, w_out, e, et, *to_cast)
    return out[0], out[1:]


def _rwkv_out_tile(h_ref, y_ref, r_ref, k_ref, v_ref, gate_ref, rk_ref, gg_ref, gb_ref, wout_ref,
                   e_ref, et_ref, o_ref):
    tm = h_ref.shape[0]
    subs = [pl.ds(s0, SUB_ROWS) for s0 in range(0, tm, SUB_ROWS)]
    each = lambda f, *lists: [f(*xs) for xs in zip(*lists)]
    reduce_ = lambda z: _head_sums_reduce(z, e_ref[...])
    expand = lambda s: _head_sums_expand(s, et_ref[...])
    y = [y_ref[rows, :] for rows in subs]
    rk = [r_ref[rows, :] * k_ref[rows, :] * rk_ref[...] for rows in subs]
    y_sum, rk_sum = each(reduce_, y), each(reduce_, rk)
    y_sum, rk_sum = each(expand, y_sum), each(expand, rk_sum)
    dy = each(lambda a, s: a - s * (1.0 / RWKV_HEAD), y, y_sum)
    var = each(expand, each(lambda a: reduce_(a * a), dy))
    for rows, dy_, var_, rk_ in zip(subs, dy, var, rk_sum):
        yn = dy_ * lax.rsqrt(var_ * (1.0 / RWKV_HEAD) + RWKV_GN_EPS) * gg_ref[...] + gb_ref[...]
        out = (yn + rk_ * v_ref[rows, :]) * _silu(gate_ref[rows, :])
        o_ref[rows, :] = h_ref[rows, :] + _mm(out, wout_ref[...])


def _hgrn_body(h_ref, ng_ref, win_ref, lbl_ref, gg_ref, wout_ref, o_ref, state_ref, *, layer):
    tm, d = h_ref.shape
    c = CHUNK
    nh = d // HGRN_HEAD

    @pl.when(pl.program_id(1) == 0)
    def _():
        state_ref[...] = jnp.zeros_like(state_ref)

    logits = lbl_ref[...]
    ex = jnp.exp(logits - jnp.max(logits, axis=0, keepdims=True))
    lb = jnp.sum(ex[1:layer + 1, :], axis=0, keepdims=True) / jnp.sum(ex, axis=0, keepdims=True)

    nc = SUB_ROWS // c
    tri2 = jnp.concatenate([_tri_ones(c)] * 2, axis=1)
    row = _iota2((c, HGRN_HEAD), 0)
    top = row < HALF
    ri, ci_ = _iota2((c, c), 0), _iota2((c, c), 1)
    diag_blocks = (ri // HALF) == (ci_ // HALF)
    causal = ri >= ci_
    each = lambda f, *lists: [f(*xs) for xs in zip(*lists)]
    where = [(slice(ci * c, (ci + 1) * c), slice(hi * HGRN_HEAD, (hi + 1) * HGRN_HEAD))
             for ci in range(nc) for hi in range(nh)]
    take = lambda z: [z[rows, cols] for rows, cols in where]
    s_t = [state_ref[hi] for hi in range(nh)]

    project = lambda x: _mm(_rmsnorm(x, ng_ref[...]), win_ref[...])
    for sub_rows, x, p in _projected_ahead(h_ref, tm, project):
        q = p[:, 0:d]
        f_pre = p[:, d:2 * d]
        v = p[:, 2 * d:3 * d]
        gate = p[:, 3 * d:4 * d]
        sig = _sigmoid(f_pre)
        log_f = jnp.log(lb + (1.0 - lb) * sig)
        k = (1.0 - lb) * (1.0 - sig)
        qs, ks, vs, gs = take(q), take(k), take(v), take(log_f)
        b = each(lambda x: _cumsum_rows_fused(tri2, x), gs)
        bl = each(lambda x: x[c - 1:c], b)
        beta = each(lambda x: x[HALF - 1:HALF], b)
        m_a = each(lambda x: x[HALF // 2:HALF // 2 + 1], b)
        m_b = each(lambda x: x[HALF + HALF // 2:HALF + HALF // 2 + 1], b)
        mid = each(lambda x, y: jnp.where(top, x, y), m_a, m_b)
        q_df = each(lambda a, x, m: a * jnp.exp(x - m), qs, b, mid)
        k_df = each(lambda a, x, m: a * jnp.exp(m - x), ks, b, mid)
        q_d = each(lambda a: a.astype(BF16), q_df)
        k_d = each(lambda a: a.astype(BF16), k_df)
        q_o = each(lambda a, y, e: jnp.where(top, 0.0, a * jnp.exp(y - e)).astype(BF16), q_df, m_b, beta)
        k_o = each(lambda a, x, e: jnp.where(top, a * jnp.exp(e - x), 0.0).astype(BF16), k_df, m_a, beta)
        q_in = each(lambda a, x, y: (a * jnp.where(top, jnp.exp(x), jnp.exp(y))).astype(BF16),
                    q_df, m_a, m_b)
        k_out = each(lambda a, x, y, e: (a * jnp.where(top, jnp.exp(e - x), jnp.exp(e - y))).astype(BF16),
                     k_df, m_a, m_b, bl)
        v_b = each(lambda a: a.astype(BF16), vs)
        att = each(lambda qd, kd, qo, ko: jnp.where(diag_blocks, jnp.where(causal, _mm_nt(qd, kd), 0.0), _mm_nt(qo, ko)),
                   q_d, k_d, q_o, k_o)
        o_intra = each(_mm, att, v_b)
        kv = each(_mm_tn, v_b, k_out)
        s_in = []
        for n in range(len(where)):
            hi = n % nh
            s_in.append(s_t[hi])
            s_t[hi] = s_t[hi] * jnp.exp(bl[n]) + kv[n]
        o = each(lambda oi, a, s: oi + _mm_nt(a, s), o_intra, q_in, s_in)
        o = each(lambda z: z * lax.rsqrt(jnp.mean(z * z, axis=-1, keepdims=True) + RMS_EPS), o)
        o_all = jnp.concatenate([jnp.concatenate(o[ci * nh:(ci + 1) * nh], axis=1) for ci in range(nc)], axis=0)
        o_all = o_all * gg_ref[...] * _silu(gate)
        o_ref[sub_rows, :] = x + _mm(o_all, wout_ref[...])
    for hi in range(nh):
        state_ref[hi] = s_t[hi]


def _hgrn_layer(h, ng, w_in, lb_logits, gn_g, w_out, *, layer, tm):
    bsz, t, d = h.shape
    row = _row_spec(tm, d)
    vec = _full_spec((1, d))
    return pl.pallas_call(
        functools.partial(_hgrn_body, layer=layer),
        grid=(bsz, t // tm),
        in_specs=[row, vec, _full_spec(w_in.shape), _full_spec(lb_logits.shape), vec, _full_spec(w_out.shape)],
        out_specs=row,
        out_shape=jax.ShapeDtypeStruct((bsz, t, d), F32),
        scratch_shapes=[pltpu.VMEM((d // HGRN_HEAD, HGRN_HEAD, HGRN_HEAD), F32)],
        compiler_params=_params(2),
        name="hgrn_layer",
    )(h, ng, w_in, lb_logits, gn_g, w_out)


def _conv_body(h_ref, ng_ref, win_ref, cw_ref, wout_ref, o_ref, carry_ref):
    tm, d = h_ref.shape

    @pl.when(pl.program_id(1) == 0)
    def _():
        carry_ref[...] = jnp.zeros_like(carry_ref)

    project = lambda x: _mm(_rmsnorm(x, ng_ref[...]), win_ref[...])
    tail = carry_ref[...]
    for rows, x, p in _projected_ahead(h_ref, tm, project):
        b_gate = p[:, 0:d]
        y = p[:, d:2 * d] * p[:, 2 * d:3 * d]
        gate = p[:, 3 * d:4 * d]
        y1 = _shift_rows(y, tail, 1)
        y2 = _shift_rows(y, tail, 2)
        tail = y[SUB_ROWS - SUBLANES:SUB_ROWS, :]
        yc = cw_ref[0:1, :] * y2 + cw_ref[1:2, :] * y1 + cw_ref[2:3, :] * y
        o_ref[rows, :] = x + _mm(b_gate * yc * _silu(gate), wout_ref[...])
    carry_ref[...] = tail


def _conv_layer(h, ng, w_in, conv_w, w_out, *, tm):
    bsz, t, d = h.shape
    row = _row_spec(tm, d)
    return pl.pallas_call(
        _conv_body,
        grid=(bsz, t // tm),
        in_specs=[row, _full_spec((1, d)), _full_spec(w_in.shape), _full_spec(conv_w.shape), _full_spec(w_out.shape)],
        out_specs=row,
        out_shape=jax.ShapeDtypeStruct((bsz, t, d), F32),
        scratch_shapes=[pltpu.VMEM((SUBLANES, d), F32)],
        compiler_params=_params(2),
        name="conv_layer",
    )(h, ng, w_in, conv_w, w_out)


def _gmlp_body(h_ref, ng_ref, win_ref, vg_ref, ws_ref, bs_ref, wout_ref, fg_ref, o_ref):
    tm, d = h_ref.shape
    c = GMLP_CHUNK
    causal = _iota2((c, c), 0) >= _iota2((c, c), 1)
    bias = bs_ref[...]
    nc = SUB_ROWS // c
    project = lambda x: _mm(_rmsnorm(x, ng_ref[...]), win_ref[...])
    for rows, x, p in _projected_ahead(h_ref, tm, project):
        u = p[:, 0:d]
        vb = _rmsnorm(p[:, d:2 * d], vg_ref[...]).astype(BF16)
        gate = p[:, 2 * d:3 * d]
        parts = []
        for gi in range(d // GMLP_GROUP):
            cols = slice(gi * GMLP_GROUP, (gi + 1) * GMLP_GROUP)
            v_g = jnp.concatenate([vb[ci * c:(ci + 1) * c, cols] for ci in range(nc)], axis=1)
            parts.append(_mm(jnp.where(causal, ws_ref[gi], 0.0), v_g))
        s = jnp.concatenate(
            [jnp.concatenate([q[:, ci * GMLP_GROUP:(ci + 1) * GMLP_GROUP] for q in parts], axis=1) + bias
             for ci in range(nc)], axis=0)
        out = x + _mm(u * s * _silu(gate), wout_ref[...])
        o_ref[rows, :] = _rmsnorm(out, fg_ref[...])


def _gmlp_layer(h, ng, w_in, v_g, w_s, bs_full, w_out, final_g, *, tm):
    bsz, t, d = h.shape
    row = _row_spec(tm, d)
    vec = _full_spec((1, d))
    return pl.pallas_call(
        _gmlp_body,
        grid=(bsz, t // tm),
        in_specs=[row, vec, _full_spec(w_in.shape), vec, _full_spec(w_s.shape), _full_spec(bs_full.shape),
                  _full_spec(w_out.shape), vec],
        out_specs=row,
        out_shape=jax.ShapeDtypeStruct((bsz, t, d), F32),
        compiler_params=_params(2),
        name="gmlp_layer",
    )(h, ng, w_in, v_g, w_s, bs_full, w_out, final_g)


def _rwkv_layer(h, ng, w_in, mu, w0, w_w2, a0, w_a2, k_k, k_a, r_k, gn_g, gn_b, w_out, to_cast=()):
    d = h.shape[-1]
    zeros = jnp.zeros((RWKV_LORA, d), F32)
    ww_pad = jnp.concatenate([w_w2, zeros], axis=0).astype(BF16)
    wa_pad = jnp.concatenate([zeros, w_a2], axis=0).astype(BF16)
    e = (jnp.arange(d)[:, None] // RWKV_HEAD == jnp.arange(LANES)[None, :]).astype(BF16)
    et = jnp.concatenate([e.T, e.T], axis=0)
    vec = lambda z: z.reshape(1, -1)
    r, k, v, g, kk, bv, gate = _rwkv_prep(h, vec(ng), w_in.astype(BF16), vec(mu), vec(w0), ww_pad, vec(a0), wa_pad,
                                          vec(k_k), vec(k_a), e, et, tm=_tile_rows("rwkv_prep"))
    return _rwkv_scan(h, r, k, v, g, kk, bv, gate, vec(r_k), vec(gn_g), vec(gn_b), w_out.astype(BF16), e, et,
                      list(to_cast), tm=_tile_rows("rwkv_scan"))


def kernel(x, norm_g, final_g, rwkv_w_in, rwkv_mu, rwkv_w0, rwkv_w_w2, rwkv_a0, rwkv_w_a2, rwkv_k_k, rwkv_k_a, rwkv_r_k, rwkv_gn_g, rwkv_gn_b, rwkv_w_out, hgrn_lb_logits, hgrn_w_in, hgrn_gn_g, hgrn_w_out, conv_w_in, conv_w, conv_w_out, gmlp_w_in, gmlp_v_g, gmlp_w_s, gmlp_b_s, gmlp_w_out):
    depth = norm_g.shape[0]
    assert depth == 4, "the fused final norm assumes the gMLP layer is the last one"
    vec = lambda z: z.reshape(1, -1)
    h = x
    later = (hgrn_w_in, hgrn_w_out, conv_w_in, conv_w_out, gmlp_w_in, gmlp_w_out)
    h, later = _rwkv_layer(h, norm_g[0], rwkv_w_in[0], rwkv_mu[0], rwkv_w0[0], rwkv_w_w2[0], rwkv_a0[0], rwkv_w_a2[0],
                           rwkv_k_k[0], rwkv_k_a[0], rwkv_r_k[0], rwkv_gn_g[0], rwkv_gn_b[0], rwkv_w_out[0],
                           to_cast=later)
    hgrn_in, hgrn_out, conv_in, conv_out, gmlp_in, gmlp_out = (z[0] for z in later)
    h = _hgrn_layer(h, vec(norm_g[1]), hgrn_in, hgrn_lb_logits, vec(hgrn_gn_g[0]), hgrn_out,
                    layer=1, tm=_tile_rows("hgrn"))
    h = _conv_layer(h, vec(norm_g[2]), conv_in, conv_w[0], conv_out, tm=_tile_rows("conv"))
    bs_full = jnp.repeat(gmlp_b_s[0].T, GMLP_GROUP, axis=1)
    h = _gmlp_layer(h, vec(norm_g[3]), gmlp_in, vec(gmlp_v_g[0]), gmlp_w_s[0].astype(BF16),
                    bs_full, gmlp_out, vec(final_g), tm=_tile_rows("gmlp"))
    return h
```

```python
import functools

import jax
import jax.numpy as jnp
from jax import lax
from jax.experimental import pallas as pl
from jax.experimental.pallas import tpu as pltpu

F32 = jnp.float32
BF16 = jnp.bfloat16

RMS_EPS = 1e-6
RWKV_HEAD = 64
RWKV_GN_EPS = 64e-5
RWKV_LORA = 64
DECAY_SCALE = 0.6065306597126334
HGRN_HEAD = 128
GMLP_CHUNK = 128
GMLP_GROUP = 128

LANES = 128
SUBLANES = 8
V7X_VMEM_BYTES = 64 * 1024 * 1024
VMEM_LIMIT = V7X_VMEM_BYTES * 7 // 8

CHUNK = 64
HALF = CHUNK // 2
SUB_ROWS = 256
SCAN_GROUP_CHUNKS = 2


def _tile_rows(name):
    return {"rwkv_prep": 2 * SUB_ROWS, "rwkv_scan": 2 * SUB_ROWS, "hgrn": 2 * SUB_ROWS, "conv": 4 * SUB_ROWS,
            "gmlp": 4 * SUB_ROWS}[name]


def _mm(a, b):
    return jnp.dot(a.astype(BF16), b.astype(BF16), preferred_element_type=F32)


def _mm_nt(a, b):
    return lax.dot_general(a.astype(BF16), b.astype(BF16), (((1,), (1,)), ((), ())),
                           preferred_element_type=F32)


def _mm_tn(a, b):
    return lax.dot_general(a.astype(BF16), b.astype(BF16), (((0,), (0,)), ((), ())),
                           preferred_element_type=F32)


def _split2(x):
    hi = x.astype(BF16)
    lo = (x - hi.astype(F32)).astype(BF16)
    return hi, lo


def _head_sums_reduce(x, e):
    return jnp.dot(x.astype(BF16), e, preferred_element_type=F32)


def _head_sums_expand(s, et2):
    hi, lo = _split2(s)
    return jnp.dot(jnp.concatenate([hi, lo], axis=1), et2, preferred_element_type=F32)


def _cumsum_rows_fused(tri2, g):
    return jnp.dot(tri2, jnp.concatenate(_split2(g), axis=0), preferred_element_type=F32)


def _sigmoid(x):
    return 0.5 * jnp.tanh(0.5 * x) + 0.5


def _silu(x):
    return x * _sigmoid(x)


def _rmsnorm(x, g):
    ms = jnp.mean(x * x, axis=-1, keepdims=True)
    return x * lax.rsqrt(ms + RMS_EPS) * g


def _iota2(shape, axis):
    return lax.broadcasted_iota(jnp.int32, shape, axis)


def _shift_rows(x, tail, n):
    rolled = pltpu.roll(x, n, 0)
    head = rolled[0:SUBLANES]
    row = _iota2(head.shape, 0)
    head = jnp.where(row < n, pltpu.roll(tail, n, 0), head)
    return jnp.concatenate([head, rolled[SUBLANES:]], axis=0)


def _tri_ones(n):
    return (_iota2((n, n), 0) >= _iota2((n, n), 1)).astype(BF16)


def _projected_ahead(h_ref, tm, project):
    subs = [pl.ds(s0, SUB_ROWS) for s0 in range(0, tm, SUB_ROWS)]
    x = h_ref[subs[0], :]
    ahead = (x, project(x))
    for i, rows in enumerate(subs):
        cur = ahead
        if i + 1 < len(subs):
            x = h_ref[subs[i + 1], :]
            ahead = (x, project(x))
        yield rows, cur[0], cur[1]


def _row_spec(tm, d):
    return pl.BlockSpec((None, tm, d), lambda b, t: (b, t, 0))


def _full_spec(shape):
    n = len(shape)
    return pl.BlockSpec(shape, lambda *_: (0,) * n)


def _params(n_axes):
    return pltpu.CompilerParams(dimension_semantics=("arbitrary",) * n_axes, vmem_limit_bytes=VMEM_LIMIT)


def _rwkv_prep_body(h_ref, ng_ref, win_ref, mu_ref, w0_ref, ww_ref, a0_ref, wa_ref, kk_ref, ka_ref,
                    e_ref, et_ref,
                    r_out, k_out, v_out, g_out, kk_out, bv_out, gate_out, carry_ref, *, d):
    tm = h_ref.shape[0]

    @pl.when(pl.program_id(1) == 0)
    def _():
        carry_ref[...] = jnp.zeros_like(carry_ref)

    for s0 in range(0, tm, SUB_ROWS):
        rows = pl.ds(s0, SUB_ROWS)
        hn = _rmsnorm(h_ref[rows, :], ng_ref[...]).astype(BF16)

        def project(lo, hi):
            p = jnp.dot(hn, win_ref[:, lo:hi], preferred_element_type=F32)
            prev = _shift_rows(p, carry_ref[:, lo:hi], 1)
            carry_ref[:, lo:hi] = p[SUB_ROWS - SUBLANES:SUB_ROWS, :]
            return p + (prev - p) * mu_ref[:, lo:hi]

        lw = project(3 * d, 3 * d + 2 * RWKV_LORA)
        k = project(d, 2 * d)
        lw = jnp.where(_iota2(lw.shape, 1) < RWKV_LORA, jnp.tanh(lw), lw)
        dw = _mm(lw, ww_ref[...])
        da = _mm(lw, wa_ref[...])
        r_out[rows, :] = project(0, d)
        kk = k * kk_ref[...]
        n2 = _head_sums_reduce(kk * kk, e_ref[...])
        v_out[rows, :] = project(2 * d, 3 * d)
        n2 = _head_sums_expand(n2, et_ref[...])
        gate_out[rows, :] = project(3 * d + 2 * RWKV_LORA, win_ref.shape[1])
        a = _sigmoid(a0_ref[...] + da)
        kk = kk * lax.rsqrt(jnp.maximum(n2, 1e-24))
        k_out[rows, :] = k * (1.0 + (a - 1.0) * ka_ref[...])
        g_out[rows, :] = -DECAY_SCALE * _sigmoid(w0_ref[...] + dw)
        kk_out[rows, :] = kk
        bv_out[rows, :] = kk * a


def _rwkv_prep(h, ng, w_in, mu, w0, ww_pad, a0, wa_pad, k_k, k_a, e, et, *, tm):
    bsz, t, d = h.shape
    cols = w_in.shape[1]
    row = _row_spec(tm, d)
    vec = _full_spec((1, d))
    out = jax.ShapeDtypeStruct((bsz, t, d), F32)
    return pl.pallas_call(
        functools.partial(_rwkv_prep_body, d=d),
        grid=(bsz, t // tm),
        in_specs=[row, vec, _full_spec(w_in.shape), _full_spec((1, cols)), vec, _full_spec(ww_pad.shape), vec,
                  _full_spec(wa_pad.shape), vec, vec, _full_spec(e.shape), _full_spec(et.shape)],
        out_specs=[row] * 7,
        out_shape=[out] * 7,
        scratch_shapes=[pltpu.VMEM((SUBLANES, cols), F32)],
        compiler_params=_params(2),
        name="rwkv_prep",
    )(h, ng, w_in, mu, w0, ww_pad, a0, wa_pad, k_k, k_a, e, et)


def _stack_heads(x, m0):
    return jnp.concatenate([jnp.where(m0, x, 0.0), jnp.where(m0, 0.0, x)], axis=0)


def _unit_lower_inverse(lows, i2, j2):
    s = 1
    eye = jnp.where(i2 == j2, 1.0, 0.0)
    t_inv = [eye for _ in lows]
    while s < CHUNK:
        keep = ((i2 // (2 * s)) == (j2 // (2 * s))) & ((i2 // s) != (j2 // s))
        subs = [jnp.where(keep, low, 0.0).astype(BF16) for low in lows]
        if s == 1:
            t_inv = [t + sub for t, sub in zip(t_inv, subs)]
        else:
            t_bf = [t.astype(BF16) for t in t_inv]
            left = [_mm(tb, sub) for tb, sub in zip(t_bf, subs)]
            t_inv = [t + _mm(lt, tb) for t, lt, tb in zip(t_inv, left, t_bf)]
        s *= 2
    return t_inv


def _rwkv_scan_body(*refs, n_cast):
    (r_ref, k_ref, v_ref, g_ref, kk_ref, bv_ref, gate_ref, h_ref, rk_ref, gg_ref, gb_ref,
     wout_ref, e_ref, et_ref) = refs[:14]
    cast_in, o_ref, cast_out = refs[14:14 + n_cast], refs[14 + n_cast], refs[15 + n_cast:15 + 2 * n_cast]
    state_ref, y_ref = refs[15 + 2 * n_cast:]
    for src, dst in zip(cast_in, cast_out):
        dst[...] = src[...].astype(BF16)
    tm = r_ref.shape[0]
    c = CHUNK
    w = 2 * RWKV_HEAD
    nc = tm // c
    n_pairs = r_ref.shape[1] // w

    @pl.when(pl.program_id(2) == 0)
    def _():
        state_ref[...] = jnp.zeros_like(state_ref)

    m0 = _iota2((c, w), 1) < RWKV_HEAD
    tri2 = jnp.concatenate([_tri_ones(c)] * 2, axis=1)
    i2 = _iota2((2 * c, w), 0)
    j2 = _iota2((2 * c, w), 1)
    same_head = (i2 // RWKV_HEAD) == (j2 // RWKV_HEAD)
    ir = _iota2((c, w), 0)
    jr = _iota2((c, w), 1) % c
    ir2 = _iota2((c, 2 * w), 0)
    jr2 = _iota2((c, 2 * w), 1) % c
    stack = functools.partial(_stack_heads, m0=m0)
    each = lambda f, *lists: [f(*xs) for xs in zip(*lists)]
    h_state = [state_ref[pi] for pi in range(n_pairs)]
    for c0 in range(0, nc, SCAN_GROUP_CHUNKS):
        where = [(pl.ds(ci * c, c), pl.ds(pi * w, w)) for ci in range(c0, min(c0 + SCAN_GROUP_CHUNKS, nc))
                 for pi in range(n_pairs)]
        load = lambda ref: [ref[rows, cols] for rows, cols in where]

        r, k, v, g, kk, bv = load(r_ref), load(k_ref), load(v_ref), load(g_ref), load(kk_ref), load(bv_ref)
        bc = each(lambda x: _cumsum_rows_fused(tri2, x), g)
        bl = each(lambda x: x[c - 1:c, :], bc)
        p_inv = each(lambda x: jnp.exp(-x), bc)
        p_end = each(lambda x, y: jnp.exp(x - y), bl, bc)
        rt = each(lambda a, x: a * jnp.exp(x), r, bc)
        at2 = each(lambda a, x, y: stack(-a * jnp.exp(x - y)).astype(BF16), kk, bc, g)
        v2 = each(lambda a: stack(a).astype(BF16), v)
        lhs = each(lambda a, x: jnp.concatenate([a, x.astype(BF16)], axis=0), at2, rt)
        rhs = each(lambda a, x, p: jnp.concatenate([stack(a * p), stack(x * p)], axis=0).astype(BF16), bv, k, p_inv)
        end = each(lambda a, x, p: jnp.concatenate([a * p, x * p], axis=0).astype(BF16), bv, k, p_end)

        prod = each(_mm_nt, lhs, rhs)
        low = each(lambda x: jnp.where(j2 < i2, x[0:2 * c, 0:w], 0.0), prod)
        a_ak = each(lambda x: jnp.where(jr < ir, x[0:c, w:2 * w] + x[c:2 * c, w:2 * w], 0.0).astype(BF16), prod)
        a_r = each(lambda x: jnp.where(jr2 <= ir2, x[2 * c:3 * c, :], 0.0).astype(BF16), prod)
        t_inv = _unit_lower_inverse(low, i2, j2)
        t_row = each(lambda t: (t[0:c, :] + t[c:2 * c, :]).astype(BF16), t_inv)
        z = each(_mm, a_ak, v2)
        w12 = each(lambda t, a, x: _mm(t, jnp.concatenate([a, stack(x).astype(BF16)], axis=1)), t_row, at2, z)
        w12b = each(lambda x: x.astype(BF16), w12)
        q12 = each(lambda a, x: _mm(a[:, 0:w], jnp.concatenate(
            [_stack_heads(x[:, 0:w], m0), _stack_heads(x[:, w:2 * w], m0)], axis=1)), a_r, w12b)
        q1 = each(lambda a, x: a + x[:, 0:w], rt, q12)
        q2 = each(lambda x, a, y: x[:, w:2 * w] + _mm(a[:, w:2 * w], y), q12, a_r, v2)
        zero_c = jnp.zeros((c, w), BF16)
        ti = each(lambda e, x, y: _mm_tn(e, jnp.concatenate(
            [x, jnp.concatenate([zero_c, y.astype(BF16)], axis=1)], axis=0)), end, w12b, v)
        trans = each(lambda x, e: jnp.where(same_head, x[:, 0:w], 0.0)
                     + jnp.where(i2 == j2, jnp.broadcast_to(jnp.exp(e), (2 * c, w)), 0.0), ti, bl)
        inject = each(lambda x: jnp.where(same_head, x[:, w:2 * w], 0.0), ti)
        q1t = each(lambda a, x: jnp.concatenate([a, x], axis=0).astype(BF16), q1, trans)

        for n, (rows, cols) in enumerate(where):
            pi = n % n_pairs
            yh = _mm(q1t[n], h_state[pi])
            y_ref[rows, cols] = yh[0:c, :] + q2[n]
            h_state[pi] = yh[c:3 * c, :] + inject[n]
    for pi in range(n_pairs):
        state_ref[pi] = h_state[pi]
    _rwkv_out_tile(h_ref, y_ref, r_ref, k_ref, v_ref, gate_ref, rk_ref, gg_ref, gb_ref, wout_ref, e_ref, et_ref, o_ref)


def _rwkv_scan(h, r, k, v, g, kk, bv, gate, r_k, gn_g, gn_b, w_out, e, et, to_cast, *, tm):
    bsz, t, d = r.shape
    w = 2 * RWKV_HEAD
    nt = t // tm
    row = pl.BlockSpec((None, tm, d), lambda b, p, i: (b, i, 0))
    vec = _full_spec((1, d))
    slabs, cast_shapes = [], []
    for z in to_cast:
        rows = z.shape[1] // (bsz * nt)
        assert rows * bsz * nt == z.shape[1] and rows % (2 * SUBLANES) == 0, z.shape
        slabs.append(pl.BlockSpec((None, rows, z.shape[2]), lambda b, p, i: (0, b * nt + i, 0)))
        cast_shapes.append(jax.ShapeDtypeStruct(z.shape, BF16))
    out = pl.pallas_call(
        functools.partial(_rwkv_scan_body, n_cast=len(to_cast)),
        grid=(bsz, 1, nt),
        in_specs=([row] * 8 + [vec, vec, vec, _full_spec(w_out.shape), _full_spec(e.shape), _full_spec(et.shape)]
                  + slabs),
        out_specs=[row] + slabs,
        out_shape=[jax.ShapeDtypeStruct((bsz, t, d), F32)] + cast_shapes,
        scratch_shapes=[pltpu.VMEM((d // w, w, w), F32), pltpu.VMEM((tm, d), F32)],
        compiler_params=_params(3),
        name="rwkv_scan",
    )(r, k, v, g, kk, bv, gate, h, r_k, gn_g, gn_b, w_out, e, et, *to_cast)
    return out[0], out[1:]


def _rwkv_out_tile(h_ref, y_ref, r_ref, k_ref, v_ref, gate_ref, rk_ref, gg_ref, gb_ref, wout_ref,
                   e_ref, et_ref, o_ref):
    tm = h_ref.shape[0]
    subs = [pl.ds(s0, SUB_ROWS) for s0 in range(0, tm, SUB_ROWS)]
    each = lambda f, *lists: [f(*xs) for xs in zip(*lists)]
    reduce_ = lambda z: _head_sums_reduce(z, e_ref[...])
    expand = lambda s: _head_sums_expand(s, et_ref[...])
    y = [y_ref[rows, :] for rows in subs]
    rk = [r_ref[rows, :] * k_ref[rows, :] * rk_ref[...] for rows in subs]
    y_sum, rk_sum = each(reduce_, y), each(reduce_, rk)
    y_sum, rk_sum = each(expand, y_sum), each(expand, rk_sum)
    dy = each(lambda a, s: a - s * (1.0 / RWKV_HEAD), y, y_sum)
    var = each(expand, each(lambda a: reduce_(a * a), dy))
    for rows, dy_, var_, rk_ in zip(subs, dy, var, rk_sum):
        yn = dy_ * lax.rsqrt(var_ * (1.0 / RWKV_HEAD) + RWKV_GN_EPS) * gg_ref[...] + gb_ref[...]
        out = (yn + rk_ * v_ref[rows, :]) * _silu(gate_ref[rows, :])
        o_ref[rows, :] = h_ref[rows, :] + _mm(out, wout_ref[...])


def _hgrn_body(h_ref, ng_ref, win_ref, lbl_ref, gg_ref, wout_ref, o_ref, state_ref, *, layer):
    tm, d = h_ref.shape
    c = CHUNK
    nh = d // HGRN_HEAD

    @pl.when(pl.program_id(1) == 0)
    def _():
        state_ref[...] = jnp.zeros_like(state_ref)

    logits = lbl_ref[...]
    ex = jnp.exp(logits - jnp.max(logits, axis=0, keepdims=True))
    lb = jnp.sum(ex[1:layer + 1, :], axis=0, keepdims=True) / jnp.sum(ex, axis=0, keepdims=True)

    nc = SUB_ROWS // c
    tri2 = jnp.concatenate([_tri_ones(c)] * 2, axis=1)
    row = _iota2((c, HGRN_HEAD), 0)
    top = row < HALF
    ri, ci_ = _iota2((c, c), 0), _iota2((c, c), 1)
    diag_blocks = (ri // HALF) == (ci_ // HALF)
    causal = ri >= ci_
    each = lambda f, *lists: [f(*xs) for xs in zip(*lists)]
    where = [(slice(ci * c, (ci + 1) * c), slice(hi * HGRN_HEAD, (hi + 1) * HGRN_HEAD))
             for ci in range(nc) for hi in range(nh)]
    take = lambda z: [z[rows, cols] for rows, cols in where]
    s_t = [state_ref[hi] for hi in range(nh)]

    project = lambda x: _mm(_rmsnorm(x, ng_ref[...]), win_ref[...])
    for sub_rows, x, p in _projected_ahead(h_ref, tm, project):
        q = p[:, 0:d]
        f_pre = p[:, d:2 * d]
        v = p[:, 2 * d:3 * d]
        gate = p[:, 3 * d:4 * d]
        sig = _sigmoid(f_pre)
        log_f = jnp.log(lb + (1.0 - lb) * sig)
        k = (1.0 - lb) * (1.0 - sig)
        qs, ks, vs, gs = take(q), take(k), take(v), take(log_f)
        b = each(lambda x: _cumsum_rows_fused(tri2, x), gs)
        bl = each(lambda x: x[c - 1:c], b)
        beta = each(lambda x: x[HALF - 1:HALF], b)
        m_a = each(lambda x: x[HALF // 2:HALF // 2 + 1], b)
        m_b = each(lambda x: x[HALF + HALF // 2:HALF + HALF // 2 + 1], b)
        mid = each(lambda x, y: jnp.where(top, x, y), m_a, m_b)
        q_df = each(lambda a, x, m: a * jnp.exp(x - m), qs, b, mid)
        k_df = each(lambda a, x, m: a * jnp.exp(m - x), ks, b, mid)
        q_d = each(lambda a: a.astype(BF16), q_df)
        k_d = each(lambda a: a.astype(BF16), k_df)
        q_o = each(lambda a, y, e: jnp.where(top, 0.0, a * jnp.exp(y - e)).astype(BF16), q_df, m_b, beta)
        k_o = each(lambda a, x, e: jnp.where(top, a * jnp.exp(e - x), 0.0).astype(BF16), k_df, m_a, beta)
        q_in = each(lambda a, x, y: (a * jnp.where(top, jnp.exp(x), jnp.exp(y))).astype(BF16),
                    q_df, m_a, m_b)
        k_out = each(lambda a, x, y, e: (a * jnp.where(top, jnp.exp(e - x), jnp.exp(e - y))).astype(BF16),
                     k_df, m_a, m_b, bl)
        v_b = each(lambda a: a.astype(BF16), vs)
        att = each(lambda qd, kd, qo, ko: jnp.where(diag_blocks, jnp.where(causal, _mm_nt(qd, kd), 0.0), _mm_nt(qo, ko)),
                   q_d, k_d, q_o, k_o)
        o_intra = each(_mm, att, v_b)
        kv = each(_mm_tn, v_b, k_out)
        s_in = []
        for n in range(len(where)):
            hi = n % nh
            s_in.append(s_t[hi])
            s_t[hi] = s_t[hi] * jnp.exp(bl[n]) + kv[n]
        o = each(lambda oi, a, s: oi + _mm_nt(a, s), o_intra, q_in, s_in)
        o = each(lambda z: z * lax.rsqrt(jnp.mean(z * z, axis=-1, keepdims=True) + RMS_EPS), o)
        o_all = jnp.concatenate([jnp.concatenate(o[ci * nh:(ci + 1) * nh], axis=1) for ci in range(nc)], axis=0)
        o_all = o_all * gg_ref[...] * _silu(gate)
        o_ref[sub_rows, :] = x + _mm(o_all, wout_ref[...])
    for hi in range(nh):
        state_ref[hi] = s_t[hi]


def _hgrn_layer(h, ng, w_in, lb_logits, gn_g, w_out, *, layer, tm):
    bsz, t, d = h.shape
    row = _row_spec(tm, d)
    vec = _full_spec((1, d))
    return pl.pallas_call(
        functools.partial(_hgrn_body, layer=layer),
        grid=(bsz, t // tm),
        in_specs=[row, vec, _full_spec(w_in.shape), _full_spec(lb_logits.shape), vec, _full_spec(w_out.shape)],
        out_specs=row,
        out_shape=jax.ShapeDtypeStruct((bsz, t, d), F32),
        scratch_shapes=[pltpu.VMEM((d // HGRN_HEAD, HGRN_HEAD, HGRN_HEAD), F32)],
        compiler_params=_params(2),
        name="hgrn_layer",
    )(h, ng, w_in, lb_logits, gn_g, w_out)


def _conv_body(h_ref, ng_ref, win_ref, cw_ref, wout_ref, o_ref, carry_ref):
    tm, d = h_ref.shape

    @pl.when(pl.program_id(1) == 0)
    def _():
        carry_ref[...] = jnp.zeros_like(carry_ref)

    project = lambda x: _mm(_rmsnorm(x, ng_ref[...]), win_ref[...])
    tail = carry_ref[...]
    for rows, x, p in _projected_ahead(h_ref, tm, project):
        b_gate = p[:, 0:d]
        y = p[:, d:2 * d] * p[:, 2 * d:3 * d]
        gate = p[:, 3 * d:4 * d]
        y1 = _shift_rows(y, tail, 1)
        y2 = _shift_rows(y, tail, 2)
        tail = y[SUB_ROWS - SUBLANES:SUB_ROWS, :]
        yc = cw_ref[0:1, :] * y2 + cw_ref[1:2, :] * y1 + cw_ref[2:3, :] * y
        o_ref[rows, :] = x + _mm(b_gate * yc * _silu(gate), wout_ref[...])
    carry_ref[...] = tail


def _conv_layer(h, ng, w_in, conv_w, w_out, *, tm):
    bsz, t, d = h.shape
    row = _row_spec(tm, d)
    return pl.pallas_call(
        _conv_body,
        grid=(bsz, t // tm),
        in_specs=[row, _full_spec((1, d)), _full_spec(w_in.shape), _full_spec(conv_w.shape), _full_spec(w_out.shape)],
        out_specs=row,
        out_shape=jax.ShapeDtypeStruct((bsz, t, d), F32),
        scratch_shapes=[pltpu.VMEM((SUBLANES, d), F32)],
        compiler_params=_params(2),
        name="conv_layer",
    )(h, ng, w_in, conv_w, w_out)


def _gmlp_body(h_ref, ng_ref, win_ref, vg_ref, ws_ref, bs_ref, wout_ref, fg_ref, o_ref):
    tm, d = h_ref.shape
    c = GMLP_CHUNK
    causal = _iota2((c, c), 0) >= _iota2((c, c), 1)
    bias = bs_ref[...]
    nc = SUB_ROWS // c
    project = lambda x: _mm(_rmsnorm(x, ng_ref[...]), win_ref[...])
    for rows, x, p in _projected_ahead(h_ref, tm, project):
        u = p[:, 0:d]
        vb = _rmsnorm(p[:, d:2 * d], vg_ref[...]).astype(BF16)
        gate = p[:, 2 * d:3 * d]
        parts = []
        for gi in range(d // GMLP_GROUP):
            cols = slice(gi * GMLP_GROUP, (gi + 1) * GMLP_GROUP)
            v_g = jnp.concatenate([vb[ci * c:(ci + 1) * c, cols] for ci in range(nc)], axis=1)
            parts.append(_mm(jnp.where(causal, ws_ref[gi], 0.0), v_g))
        s = jnp.concatenate(
            [jnp.concatenate([q[:, ci * GMLP_GROUP:(ci + 1) * GMLP_GROUP] for q in parts], axis=1) + bias
             for ci in range(nc)], axis=0)
        out = x + _mm(u * s * _silu(gate), wout_ref[...])
        o_ref[rows, :] = _rmsnorm(out, fg_ref[...])


def _gmlp_layer(h, ng, w_in, v_g, w_s, bs_full, w_out, final_g, *, tm):
    bsz, t, d = h.shape
    row = _row_spec(tm, d)
    vec = _full_spec((1, d))
    return pl.pallas_call(
        _gmlp_body,
        grid=(bsz, t // tm),
        in_specs=[row, vec, _full_spec(w_in.shape), vec, _full_spec(w_s.shape), _full_spec(bs_full.shape),
                  _full_spec(w_out.shape), vec],
        out_specs=row,
        out_shape=jax.ShapeDtypeStruct((bsz, t, d), F32),
        compiler_params=_params(2),
        name="gmlp_layer",
    )(h, ng, w_in, v_g, w_s, bs_full, w_out, final_g)


def _rwkv_layer(h, ng, w_in, mu, w0, w_w2, a0, w_a2, k_k, k_a, r_k, gn_g, gn_b, w_out, to_cast=()):
    d = h.shape[-1]
    zeros = jnp.zeros((RWKV_LORA, d), F32)
    ww_pad = jnp.concatenate([w_w2, zeros], axis=0).astype(BF16)
    wa_pad = jnp.concatenate([zeros, w_a2], axis=0).astype(BF16)
    e = (jnp.arange(d)[:, None] // RWKV_HEAD == jnp.arange(LANES)[None, :]).astype(BF16)
    et = jnp.concatenate([e.T, e.T], axis=0)
    vec = lambda z: z.reshape(1, -1)
    r, k, v, g, kk, bv, gate = _rwkv_prep(h, vec(ng), w_in.astype(BF16), vec(mu), vec(w0), ww_pad, vec(a0), wa_pad,
                                          vec(k_k), vec(k_a), e, et, tm=_tile_rows("rwkv_prep"))
    return _rwkv_scan(h, r, k, v, g, kk, bv, gate, vec(r_k), vec(gn_g), vec(gn_b), w_out.astype(BF16), e, et,
                      list(to_cast), tm=_tile_rows("rwkv_scan"))


def kernel(x, norm_g, final_g, rwkv_w_in, rwkv_mu, rwkv_w0, rwkv_w_w2, rwkv_a0, rwkv_w_a2, rwkv_k_k, rwkv_k_a, rwkv_r_k, rwkv_gn_g, rwkv_gn_b, rwkv_w_out, hgrn_lb_logits, hgrn_w_in, hgrn_gn_g, hgrn_w_out, conv_w_in, conv_w, conv_w_out, gmlp_w_in, gmlp_v_g, gmlp_w_s, gmlp_b_s, gmlp_w_out):
    depth = norm_g.shape[0]
    assert depth == 4, "the fused final norm assumes the gMLP layer is the last one"
    vec = lambda z: z.reshape(1, -1)
    h = x
    later = (hgrn_w_in, hgrn_w_out, conv_w_in, conv_w_out, gmlp_w_in, gmlp_w_out)
    h, later = _rwkv_layer(h, norm_g[0], rwkv_w_in[0], rwkv_mu[0], rwkv_w0[0], rwkv_w_w2[0], rwkv_a0[0], rwkv_w_a2[0],
                           rwkv_k_k[0], rwkv_k_a[0], rwkv_r_k[0], rwkv_gn_g[0], rwkv_gn_b[0], rwkv_w_out[0],
                           to_cast=later)
    hgrn_in, hgrn_out, conv_in, conv_out, gmlp_in, gmlp_out = (z[0] for z in later)
    h = _hgrn_layer(h, vec(norm_g[1]), hgrn_in, hgrn_lb_logits, vec(hgrn_gn_g[0]), hgrn_out,
                    layer=1, tm=_tile_rows("hgrn"))
    h = _conv_layer(h, vec(norm_g[2]), conv_in, conv_w[0], conv_out, tm=_tile_rows("conv"))
    bs_full = jnp.repeat(gmlp_b_s[0].T, GMLP_GROUP, axis=1)
    h = _gmlp_layer(h, vec(norm_g[3]), gmlp_in, vec(gmlp_v_g[0]), gmlp_w_s[0].astype(BF16),
                    bs_full, gmlp_out, vec(final_g), tm=_tile_rows("gmlp"))
    return h
```
